```python
import math
import jax, jax.numpy as jnp
from jax import lax
import numpy as np

D_MODEL = 4096
BATCH = 8
SEQ = 2048
DEPTH = 1
DEC_BATCH = 8
DEC_SEQ = 16
PAST_LEN = 2048

CHUNK = 64
SSD_HEAD_DIM = 64
SSD_HEADS = (D_MODEL // 2) // SSD_HEAD_DIM
SSD_WIDTH = SSD_HEADS * SSD_HEAD_DIM
SSD_GROUPS = 4
SSD_HEADS_PER_GROUP = SSD_HEADS // SSD_GROUPS
SSD_STATE = 128
SSD_CONV = 4
SSD_CONV_CH = SSD_WIDTH + 2 * SSD_GROUPS * SSD_STATE
ATTN_HEAD_DIM = 64
ATTN_HEADS = (D_MODEL // 2) // ATTN_HEAD_DIM
ATTN_KV_HEADS = 4
ATTN_Q_PER_KV = ATTN_HEADS // ATTN_KV_HEADS
ATTN_WIDTH = ATTN_HEADS * ATTN_HEAD_DIM
KV_WIDTH = ATTN_KV_HEADS * ATTN_HEAD_DIM
WINDOW = 128
WINDOW_CHUNKS = WINDOW // CHUNK
MIX_WIDTH = SSD_WIDTH + ATTN_WIDTH
D_IN_PROJ = SSD_WIDTH + SSD_CONV_CH + SSD_HEADS + ATTN_WIDTH + 2 * KV_WIDTH
D_FF = -(-8 * D_MODEL // (3 * 256)) * 256
N_MOD = 6
EPS = 1e-6

kernel_name = "hybrid_ssd_swa_stream_step"


def rms_norm(x, w):
    xf = x.astype(jnp.float32)
    y = xf * lax.rsqrt(jnp.mean(xf * xf, axis=-1, keepdims=True) + EPS)
    return (y * w.astype(jnp.float32)).astype(x.dtype)


def gated_rms_norm(y, z, w):
    g = y.astype(jnp.float32) * jax.nn.silu(z.astype(jnp.float32))
    shp = g.shape
    g = g.reshape(shp[:-1] + (SSD_GROUPS, SSD_WIDTH // SSD_GROUPS))
    g = g * lax.rsqrt(jnp.mean(g * g, axis=-1, keepdims=True) + EPS)
    return (g.reshape(shp) * w.astype(jnp.float32)).astype(z.dtype)


def segsum(a):
    t = a.shape[-1]
    cs = jnp.cumsum(a, axis=-1)
    diff = cs[..., :, None] - cs[..., None, :]
    return jnp.where(jnp.tril(jnp.ones((t, t), dtype=bool)), diff, -jnp.inf)


def ssd_scan(x, dt, a_neg, b_in, c_in, init_state):
    bsz, seq = x.shape[0], x.shape[1]
    blk = min(CHUNK, seq)
    nb = seq // blk
    g, k, p, n = SSD_GROUPS, SSD_HEADS_PER_GROUP, SSD_HEAD_DIM, SSD_STATE
    xbar = (x.astype(jnp.float32) * dt[..., None]).reshape(bsz, nb, blk, g, k, p)
    a = (dt * a_neg).reshape(bsz, nb, blk, g, k).transpose(0, 3, 4, 1, 2)
    a_cs = jnp.cumsum(a, axis=-1)
    decay_in = jnp.exp(segsum(a))
    bb = b_in.astype(jnp.float32).reshape(bsz, nb, blk, g, n)
    cb = c_in.astype(jnp.float32).reshape(bsz, nb, blk, g, n)
    cbt = jnp.einsum("bclgn,bcsgn->bcgls", cb, bb)
    y_diag = jnp.einsum("bcgls,bgkcls,bcsgkp->bclgkp", cbt, decay_in, xbar)
    decay_to_end = jnp.exp(a_cs[..., -1:] - a_cs)
    blk_states = jnp.einsum("bcsgn,bgkcs,bcsgkp->bcgkpn", bb, decay_to_end, xbar)
    init = init_state.astype(jnp.float32).reshape(bsz, 1, g, k, p, n)
    blk_states = jnp.concatenate([init, blk_states], axis=1)
    blk_decay = jnp.exp(segsum(jnp.pad(a_cs[..., -1], ((0, 0), (0, 0), (0, 0), (1, 0)))))
    states = jnp.einsum("bgkzc,bcgkpn->bzgkpn", blk_decay, blk_states)
    y_off = jnp.einsum("bclgn,bcgkpn,bgkcl->bclgkp", cb, states[:, :-1], jnp.exp(a_cs))
    y = (y_diag + y_off).reshape(bsz, seq, SSD_HEADS, SSD_HEAD_DIM)
    final = states[:, -1].reshape(bsz, SSD_HEADS, SSD_HEAD_DIM, SSD_STATE)
    return y, final


def causal_conv(xbc, left, w, bias):
    xp = jnp.concatenate([left.astype(xbc.dtype), xbc], axis=1)
    out = lax.conv_general_dilated(xp, w.astype(xp.dtype)[:, None, :], (1,), "VALID",
                                   dimension_numbers=("NWC", "WIO", "NWC"),
                                   feature_group_count=SSD_CONV_CH)
    return jax.nn.silu(out + bias), xp[:, xp.shape[1] - (SSD_CONV - 1):]


def alibi_slopes():
    return jnp.exp2(-8.0 * jnp.arange(1, ATTN_HEADS + 1, dtype=jnp.float32) / ATTN_HEADS)


def band_mask(q_pos, k_pos):
    qc = q_pos[..., :, None] // CHUNK
    kc = k_pos[..., None, :] // CHUNK
    return (k_pos[..., None, :] >= 0) & (kc <= qc) & (qc - kc <= WINDOW_CHUNKS)


def chunk_window_attend(q, k, v, q_pos, k_pos, sinks):
    scale = ATTN_HEAD_DIM ** -0.5
    s = jnp.einsum("bnqkgd,bnskd->bnkgqs", q, k).astype(jnp.float32) * scale
    slopes = alibi_slopes().reshape(ATTN_KV_HEADS, ATTN_Q_PER_KV)[None, None, :, :, None, None]
    dist = jnp.abs(q_pos[..., :, None] - k_pos[..., None, :]).astype(jnp.float32)[None, :, None, None]
    s = jnp.where(band_mask(q_pos, k_pos)[None, :, None, None], s - slopes * dist, -jnp.inf)
    sink = sinks.astype(jnp.float32).reshape(ATTN_KV_HEADS, ATTN_Q_PER_KV)[None, None, :, :, None, None]
    m = jnp.maximum(jnp.max(s, axis=-1, keepdims=True), sink)
    p = jnp.exp(s - m)
    p = p / (jnp.sum(p, axis=-1, keepdims=True) + jnp.exp(sink - m))
    return jnp.einsum("bnkgqs,bnskd->bnqkgd", p.astype(v.dtype), v)


def attn_prompt(q, k, v, sinks):
    bsz, seq = q.shape[0], q.shape[1]
    nc = seq // CHUNK
    qb = q.reshape(bsz, nc, CHUNK, ATTN_KV_HEADS, ATTN_Q_PER_KV, ATTN_HEAD_DIM)
    pad = ((0, 0), (WINDOW_CHUNKS * CHUNK, 0), (0, 0), (0, 0))
    kp = jnp.pad(k, pad).reshape(bsz, nc + WINDOW_CHUNKS, CHUNK, ATTN_KV_HEADS, ATTN_HEAD_DIM)
    vp = jnp.pad(v, pad).reshape(bsz, nc + WINDOW_CHUNKS, CHUNK, ATTN_KV_HEADS, ATTN_HEAD_DIM)
    kb = jnp.concatenate([kp[:, j:j + nc] for j in range(WINDOW_CHUNKS + 1)], axis=2)
    vb = jnp.concatenate([vp[:, j:j + nc] for j in range(WINDOW_CHUNKS + 1)], axis=2)
    q_pos = jnp.arange(seq).reshape(nc, CHUNK)
    k_pos = (jnp.arange(nc)[:, None] - WINDOW_CHUNKS) * CHUNK + jnp.arange((WINDOW_CHUNKS + 1) * CHUNK)[None, :]
    o = chunk_window_attend(qb, kb, vb, q_pos, k_pos, sinks)
    return o.reshape(bsz, seq, ATTN_WIDTH)


def attn_sample(q, k, v, cache_k, cache_v, sinks):
    bsz, seq = q.shape[0], q.shape[1]
    rows = cache_k.shape[1]
    qb = q.reshape(bsz, 1, seq, ATTN_KV_HEADS, ATTN_Q_PER_KV, ATTN_HEAD_DIM)
    kb = jnp.concatenate([cache_k.astype(k.dtype), k], axis=1)[:, None]
    vb = jnp.concatenate([cache_v.astype(v.dtype), v], axis=1)[:, None]
    q_pos = (PAST_LEN + jnp.arange(seq))[None]
    k_pos = jnp.concatenate([PAST_LEN - rows + jnp.arange(rows), PAST_LEN + jnp.arange(seq)])[None]
    o = chunk_window_attend(qb, kb, vb, q_pos, k_pos, sinks)
    return o.reshape(bsz, seq, ATTN_WIDTH)


def token_mixer(h, lp, ssd_init, conv_left, cache_kv):
    bsz, seq = h.shape[0], h.shape[1]
    proj = h @ lp["w_in"]
    o1 = SSD_WIDTH
    o2 = o1 + SSD_CONV_CH
    o3 = o2 + SSD_HEADS
    o4 = o3 + ATTN_WIDTH
    o5 = o4 + KV_WIDTH
    z, xbc, dt_raw, q, k, v = jnp.split(proj, [o1, o2, o3, o4, o5], axis=-1)
    xbc, conv_state = causal_conv(xbc, conv_left, lp["conv_w"], lp["conv_b"])
    xs, b_ssm, c_ssm = jnp.split(xbc, [SSD_WIDTH, SSD_WIDTH + SSD_GROUPS * SSD_STATE], axis=-1)
    dt = jax.nn.softplus(dt_raw.astype(jnp.float32) + lp["dt_bias"].astype(jnp.float32))
    a_neg = -jnp.exp(lp["a_log"].astype(jnp.float32))
    xs = xs.reshape(bsz, seq, SSD_HEADS, SSD_HEAD_DIM)
    y, ssd_state = ssd_scan(xs, dt, a_neg,
                            b_ssm.reshape(bsz, seq, SSD_GROUPS, SSD_STATE),
                            c_ssm.reshape(bsz, seq, SSD_GROUPS, SSD_STATE), ssd_init)
    y = y + xs.astype(jnp.float32) * lp["d_skip"].astype(jnp.float32)[:, None]
    y = gated_rms_norm(y.reshape(bsz, seq, SSD_WIDTH), z, lp["ssd_norm_w"])
    q = rms_norm(q.reshape(bsz, seq, ATTN_HEADS, ATTN_HEAD_DIM), lp["q_norm_w"])
    k = rms_norm(k.reshape(bsz, seq, ATTN_KV_HEADS, ATTN_HEAD_DIM), lp["k_norm_w"])
    v = v.reshape(bsz, seq, ATTN_KV_HEADS, ATTN_HEAD_DIM)
    if cache_kv is None:
        o = attn_prompt(q, k, v, lp["sinks"])
        keep = min(WINDOW, seq)
        k_state, v_state = k[:, seq - keep:], v[:, seq - keep:]
    else:
        o = attn_sample(q, k, v, cache_kv[0], cache_kv[1], lp["sinks"])
        k_state, v_state = k, v
    out = jnp.concatenate([y, o.astype(y.dtype)], axis=-1) @ lp["w_out"]
    return out, ssd_state.astype(h.dtype), conv_state, k_state, v_state


def layer(x, c, lp, ssd_init, conv_left, cache_kv):
    mod = (jax.nn.silu(c) @ lp["w_ada"] + lp["b_ada"]).reshape(c.shape[0], N_MOD, 1, D_MODEL)
    sh1, sc1, g1, sh2, sc2, g2 = [mod[:, i] for i in range(N_MOD)]
    h = rms_norm(x, lp["g_mix"]) * (1.0 + sc1) + sh1
    mix, ssd_s, conv_s, k_s, v_s = token_mixer(h, lp, ssd_init, conv_left, cache_kv)
    x = x + g1 * mix
    h = rms_norm(x, lp["g_ffn"]) * (1.0 + sc2) + sh2
    gate, up = jnp.split(h @ lp["w_gate_up"], 2, axis=-1)
    x = x + g2 * ((jax.nn.silu(gate) * up) @ lp["w_down"])
    return x, ssd_s, conv_s, k_s, v_s


def setup_inputs(seed: int = 0) -> dict:
    key = jax.random.key(seed)
    ks = jax.random.split(key, 26)
    f32 = jnp.float32

    def nrm(k, shape, scale):
        return jax.random.normal(k, shape, f32) * scale

    rows = min(WINDOW, PAST_LEN)
    dt0 = jnp.exp(jax.random.uniform(ks[14], (DEPTH, SSD_HEADS), f32, math.log(1e-3), math.log(1e-1)))
    return {
        "x_prompt": nrm(ks[0], (BATCH, SEQ, D_MODEL), 1.0),
        "x_sample": nrm(ks[1], (DEC_BATCH, DEC_SEQ, D_MODEL), 1.0),
        "state_ssd": nrm(ks[2], (DEPTH, DEC_BATCH, SSD_HEADS, SSD_HEAD_DIM, SSD_STATE), 0.1),
        "state_conv": nrm(ks[3], (DEPTH, DEC_BATCH, SSD_CONV - 1, SSD_CONV_CH), 1.0),
        "cache_k": nrm(ks[4], (DEPTH, DEC_BATCH, rows, ATTN_KV_HEADS, ATTN_HEAD_DIM), 1.0),
        "cache_v": nrm(ks[5], (DEPTH, DEC_BATCH, rows, ATTN_KV_HEADS, ATTN_HEAD_DIM), 1.0),
        "c_prompt": nrm(ks[6], (BATCH, D_MODEL), 1.0),
        "c_sample": nrm(ks[7], (DEC_BATCH, D_MODEL), 1.0),
        "w_ada": nrm(ks[8], (DEPTH, D_MODEL, N_MOD * D_MODEL), 0.5 * D_MODEL ** -0.5),
        "b_ada": nrm(ks[9], (DEPTH, N_MOD * D_MODEL), 0.01),
        "g_mix": 1.0 + nrm(ks[10], (DEPTH, D_MODEL), 0.02),
        "w_in": nrm(ks[11], (DEPTH, D_MODEL, D_IN_PROJ), D_MODEL ** -0.5),
        "conv_w": nrm(ks[12], (DEPTH, SSD_CONV, SSD_CONV_CH), SSD_CONV ** -0.5),
        "conv_b": nrm(ks[13], (DEPTH, SSD_CONV_CH), 0.01),
        "dt_bias": dt0 + jnp.log(-jnp.expm1(-dt0)),
        "a_log": jnp.log(jax.random.uniform(ks[15], (DEPTH, SSD_HEADS), f32, 1.0, 16.0)),
        "d_skip": 1.0 + nrm(ks[16], (DEPTH, SSD_HEADS), 0.1),
        "ssd_norm_w": 1.0 + nrm(ks[17], (DEPTH, SSD_WIDTH), 0.02),
        "q_norm_w": 1.0 + nrm(ks[18], (DEPTH, ATTN_HEAD_DIM), 0.02),
        "k_norm_w": 1.0 + nrm(ks[19], (DEPTH, ATTN_HEAD_DIM), 0.02),
        "sinks": nrm(ks[20], (DEPTH, ATTN_HEADS), 0.5),
        "w_out": nrm(ks[21], (DEPTH, MIX_WIDTH, D_MODEL), MIX_WIDTH ** -0.5),
        "g_ffn": 1.0 + nrm(ks[22], (DEPTH, D_MODEL), 0.02),
        "w_gate_up": nrm(ks[23], (DEPTH, D_MODEL, 2 * D_FF), D_MODEL ** -0.5),
        "w_down": nrm(ks[24], (DEPTH, D_FF, D_MODEL), D_FF ** -0.5),
    }


def reference(x_prompt, x_sample, state_ssd, state_conv, cache_k, cache_v, c_prompt, c_sample,
              w_ada, b_ada, g_mix, w_in, conv_w, conv_b, dt_bias, a_log, d_skip, ssd_norm_w,
              q_norm_w, k_norm_w, sinks, w_out, g_ffn, w_gate_up, w_down):
    yp, ys = x_prompt, x_sample
    bp = x_prompt.shape[0]
    ssd_p, conv_p, k_p, v_p = [], [], [], []
    ssd_s, conv_s, k_s, v_s = [], [], [], []
    for l in range(DEPTH):
        lp = dict(w_ada=w_ada[l], b_ada=b_ada[l], g_mix=g_mix[l], w_in=w_in[l], conv_w=conv_w[l],
                  conv_b=conv_b[l], dt_bias=dt_bias[l], a_log=a_log[l], d_skip=d_skip[l],
                  ssd_norm_w=ssd_norm_w[l], q_norm_w=q_norm_w[l], k_norm_w=k_norm_w[l],
                  sinks=sinks[l], w_out=w_out[l], g_ffn=g_ffn[l], w_gate_up=w_gate_up[l],
                  w_down=w_down[l])
        zero_ssd = jnp.zeros((bp, SSD_HEADS, SSD_HEAD_DIM, SSD_STATE), x_prompt.dtype)
        zero_conv = jnp.zeros((bp, SSD_CONV - 1, SSD_CONV_CH), x_prompt.dtype)
        yp, s1, s2, s3, s4 = layer(yp, c_prompt, lp, zero_ssd, zero_conv, None)
        ys, t1, t2, t3, t4 = layer(ys, c_sample, lp, state_ssd[l], state_conv[l], (cache_k[l], cache_v[l]))
        ssd_p.append(s1); conv_p.append(s2); k_p.append(s3); v_p.append(s4)
        ssd_s.append(t1); conv_s.append(t2); k_s.append(t3); v_s.append(t4)
    new_ssd_prompt = jnp.stack(ssd_p, axis=0)
    new_conv_prompt = jnp.stack(conv_p, axis=0)
    new_k_prompt = jnp.stack(k_p, axis=0)
    new_v_prompt = jnp.stack(v_p, axis=0)
    new_ssd_sample = jnp.stack(ssd_s, axis=0)
    new_conv_sample = jnp.stack(conv_s, axis=0)
    new_k_sample = jnp.stack(k_s, axis=0)
    new_v_sample = jnp.stack(v_s, axis=0)
    return (yp, ys, new_ssd_prompt, new_conv_prompt, new_k_prompt, new_v_prompt,
            new_ssd_sample, new_conv_sample, new_k_sample, new_v_sample)
```

```python
import functools

import jax
import jax.numpy as jnp
from jax import lax
from jax.experimental import pallas as pl
from jax.experimental.pallas import tpu as pltpu

CHUNK = 64
HEAD_DIM = 64
SSD_GROUPS = 4
SSD_STATE = 128
SSD_CONV = 4
KV_HEADS = 4
WINDOW = 128
N_MOD = 6
EPS = 1e-6

LANES = 128
SUBLANES = 8
VMEM_LIMIT_BYTES = 56 * 1024 * 1024

F32 = jnp.float32
BF16 = jnp.bfloat16
NEG_INF = float("-inf")


def _cparams(*sem):
    return pltpu.CompilerParams(dimension_semantics=sem, vmem_limit_bytes=VMEM_LIMIT_BYTES)


def _pick(n, candidates):
    for c in candidates:
        if c <= n and n % c == 0:
            return c
    return n


def _round_up(n, m):
    return -(-n // m) * m


def _silu(x):
    return x * jax.nn.sigmoid(x)


def _dot(a, b):
    return jnp.dot(a, b, preferred_element_type=F32)


def _dot_nt(a, b):
    return lax.dot_general(a, b, (((1,), (1,)), ((), ())), preferred_element_type=F32)


def _dot_tn(a, b):
    return lax.dot_general(a, b, (((0,), (0,)), ((), ())), preferred_element_type=F32)


def _split3(v):
    hi = v.astype(BF16)
    r1 = v - hi.astype(F32)
    mid = r1.astype(BF16)
    lo = (r1 - mid.astype(F32)).astype(BF16)
    return hi, mid, lo


def _dot01_right(v, m01):
    hi, mid, lo = _split3(v)
    return _dot(hi, m01) + (_dot(mid, m01) + _dot(lo, m01))


def _dot01_left(m01, v):
    hi, mid, lo = _split3(v)
    return _dot(m01, hi) + (_dot(m01, mid) + _dot(m01, lo))


def _ada_kernel(c_ref, w_ref, b_ref, o_ref):
    a = _silu(c_ref[...]).astype(BF16)
    o_ref[...] = _dot(a, w_ref[...].astype(BF16)) + b_ref[...]


def _ada(c, w, b):
    r, d = c.shape
    n = w.shape[1]
    tn = _pick(n, (512, 256, 128))
    return pl.pallas_call(
        _ada_kernel,
        grid=(n // tn,),
        in_specs=[pl.BlockSpec((r, d), lambda j: (0, 0)),
                  pl.BlockSpec((d, tn), lambda j: (0, j)),
                  pl.BlockSpec((1, tn), lambda j: (0, j))],
        out_specs=pl.BlockSpec((r, tn), lambda j: (0, j)),
        out_shape=jax.ShapeDtypeStruct((r, n), F32),
        compiler_params=_cparams("arbitrary"),
        name="ada",
    )(c, w, b.reshape(1, n))


def _norm_kernel(x_ref, g_ref, sc_ref, sh_ref, o_ref):
    x = x_ref[...]
    r = lax.rsqrt(jnp.mean(x * x, axis=-1, keepdims=True) + EPS)
    y = x * r * g_ref[...]
    o_ref[...] = (y * (1.0 + sc_ref[...]) + sh_ref[...]).astype(o_ref.dtype)


def _norm_mod(x, g, sc, sh, seq, boff):
    m, d = x.shape
    tm = _pick(seq, (256, 128, 64, 32, 16, 8))
    per = seq // tm
    mod_spec = pl.BlockSpec((None, 1, d), lambda i: (boff + i // per, 0, 0))
    return pl.pallas_call(
        _norm_kernel,
        grid=(m // tm,),
        in_specs=[pl.BlockSpec((tm, d), lambda i: (i, 0)),
                  pl.BlockSpec((1, d), lambda i: (0, 0)),
                  mod_spec, mod_spec],
        out_specs=pl.BlockSpec((tm, d), lambda i: (i, 0)),
        out_shape=jax.ShapeDtypeStruct((m, d), BF16),
        compiler_params=_cparams("arbitrary"),
        name="norm_mod",
    )(x, g.reshape(1, d), sc, sh)


def _mm_kernel(a_ref, w_ref, o_ref):
    o_ref[...] = _dot(a_ref[...], w_ref[...]).astype(o_ref.dtype)


def _mm(a, w, out_dtype, tn_candidates):
    m, k = a.shape
    n = w.shape[1]
    tm = _pick(m, (1024, 512, 256, 128))
    tn = _pick(n, tn_candidates)
    return pl.pallas_call(
        _mm_kernel,
        grid=(m // tm, n // tn),
        in_specs=[pl.BlockSpec((tm, k), lambda i, j: (i, 0)),
                  pl.BlockSpec((k, tn), lambda i, j: (0, j))],
        out_specs=pl.BlockSpec((tm, tn), lambda i, j: (i, j)),
        out_shape=jax.ShapeDtypeStruct((m, n), out_dtype),
        compiler_params=_cparams("arbitrary", "arbitrary"),
        name="mm",
    )(a, w)


def _mm_out_kernel(y_ref, o_ref, wa_ref, wb_ref, x_ref, g_ref, out_ref):
    acc = _dot(y_ref[...], wa_ref[...]) + _dot(o_ref[...], wb_ref[...])
    out_ref[...] = x_ref[...] + g_ref[...] * acc


def _gate_operand(gate, tm, tn, seq, boff, nb):
    if seq % tm == 0:
        per = seq // tm
        return gate, pl.BlockSpec((None, 1, tn), lambda i, j, *_: (boff + i // per, 0, j))
    rows = jnp.repeat(gate[boff:boff + nb, 0], seq, axis=0)
    return rows, pl.BlockSpec((tm, tn), lambda i, j, *_: (i, j))


def _mm_out(y, o, w, x, gate, seq, boff):
    m, wd = y.shape
    d = w.shape[1]
    tm = _pick(m, (1024, 512, 256, 128))
    tn = _pick(d, (512, 256, 128))
    gate, gate_spec = _gate_operand(gate, tm, tn, seq, boff, m // seq)
    return pl.pallas_call(
        _mm_out_kernel,
        grid=(m // tm, d // tn),
        in_specs=[pl.BlockSpec((tm, wd), lambda i, j: (i, 0)),
                  pl.BlockSpec((tm, wd), lambda i, j: (i, 0)),
                  pl.BlockSpec((wd, tn), lambda i, j: (0, j)),
                  pl.BlockSpec((wd, tn), lambda i, j: (1, j)),
                  pl.BlockSpec((tm, tn), lambda i, j: (i, j)),
                  gate_spec],
        out_specs=pl.BlockSpec((tm, tn), lambda i, j: (i, j)),
        out_shape=jax.ShapeDtypeStruct((m, d), F32),
        compiler_params=_cparams("arbitrary", "arbitrary"),
        name="mm_out",
    )(y, o, w, w, x, gate)


def _mm_gu_kernel(h_ref, wg_ref, wu_ref, o_ref):
    h = h_ref[...]
    g = _dot(h, wg_ref[...])
    u = _dot(h, wu_ref[...])
    o_ref[...] = (_silu(g) * u).astype(o_ref.dtype)


def _mm_gate_up(h, w, ff):
    m, d = h.shape
    tm = _pick(m, (1024, 512, 256, 128))
    tn = _pick(ff, (512, 256, 128))
    nj = ff // tn
    return pl.pallas_call(
        _mm_gu_kernel,
        grid=(m // tm, nj),
        in_specs=[pl.BlockSpec((tm, d), lambda i, j: (i, 0)),
                  pl.BlockSpec((d, tn), lambda i, j: (0, j)),
                  pl.BlockSpec((d, tn), lambda i, j: (0, nj + j))],
        out_specs=pl.BlockSpec((tm, tn), lambda i, j: (i, j)),
        out_shape=jax.ShapeDtypeStruct((m, ff), BF16),
        compiler_params=_cparams("arbitrary", "arbitrary"),
        name="mm_gate_up",
    )(h, w, w)


def _mm_down_kernel(a_ref, w_ref, x_ref, g_ref, o_ref, acc_ref):
    k = pl.program_id(2)

    @pl.when(k == 0)
    def _():
        acc_ref[...] = jnp.zeros_like(acc_ref)

    acc_ref[...] += _dot(a_ref[...], w_ref[...])

    @pl.when(k == pl.num_programs(2) - 1)
    def _():
        o_ref[...] = x_ref[...] + g_ref[...] * acc_ref[...]


def _mm_down(a, w, x, gate, seq, boff):
    m, f = a.shape
    d = w.shape[1]
    tm = _pick(m, (1024, 512, 256, 128))
    tn = _pick(d, (1024, 512, 256, 128))
    tk = _pick(f, (2816, 2048, 1408, 1024, 512, 256, 128))
    gate, gate_spec = _gate_operand(gate, tm, tn, seq, boff, m // seq)
    return pl.pallas_call(
        _mm_down_kernel,
        grid=(m // tm, d // tn, f // tk),
        in_specs=[pl.BlockSpec((tm, tk), lambda i, j, k: (i, k)),
                  pl.BlockSpec((tk, tn), lambda i, j, k: (k, j)),
                  pl.BlockSpec((tm, tn), lambda i, j, k: (i, j)),
                  gate_spec],
        out_specs=pl.BlockSpec((tm, tn), lambda i, j, k: (i, j)),
        out_shape=jax.ShapeDtypeStruct((m, d), F32),
        scratch_shapes=[pltpu.VMEM((tm, tn), F32)],
        compiler_params=_cparams("arbitrary", "arbitrary", "arbitrary"),
        name="mm_down",
    )(a, w, x, gate)


def _ssd_kernel(*refs, valid, hpg, has_init):
    if has_init:
        (z_ref, x_ref, b_ref, c_ref, dt_ref, convl_ref, init_ref, cw_ref, cb_ref, dtb_ref, alog_ref,
         dskip_ref, nw_ref, sel_ref, y_ref, state_ref, convs_ref, xpad, st) = refs
    else:
        (z_ref, x_ref, b_ref, c_ref, dt_ref, convl_ref, cw_ref, cb_ref, dtb_ref, alog_ref,
         dskip_ref, nw_ref, sel_ref, y_ref, state_ref, convs_ref, xpad, st) = refs
        init_ref = None
    c = pl.program_id(1)
    width = x_ref.shape[1]
    gn = SSD_GROUPS * SSD_STATE
    gw = hpg * HEAD_DIM
    L = CHUNK
    pre = SUBLANES
    keep = SSD_CONV - 1

    @pl.when(c == 0)
    def _():
        xpad[...] = jnp.zeros_like(xpad)
        xpad[pre - keep:pre, :] = convl_ref[...]
        if has_init:
            st[...] = init_ref[...].T
        else:
            st[...] = jnp.zeros_like(st)

    xpad[pre:pre + valid, 0:width] = x_ref[...].astype(F32)
    xpad[pre:pre + valid, width:width + gn] = b_ref[...].astype(F32)
    xpad[pre:pre + valid, width + gn:width + 2 * gn] = c_ref[...].astype(F32)

    acc = xpad[pre - keep:pre - keep + L, :] * cw_ref[0:1, :] + cb_ref[...]
    for k in range(1, SSD_CONV):
        acc = acc + xpad[pre - keep + k:pre - keep + k + L, :] * cw_ref[k:k + 1, :]
    xbc = _silu(acc)
    xs = xbc[:, 0:width]

    new_left = xpad[pre + valid - keep:pre + valid, :]
    convs_ref[...] = new_left
    xpad[pre - keep:pre, :] = new_left

    row = lax.broadcasted_iota(jnp.int32, (L, LANES), 0)
    if valid < L:
        dt_raw = jnp.concatenate([dt_ref[...], jnp.zeros((L - valid, LANES), F32)], axis=0)
    else:
        dt_raw = dt_ref[...]
    u = dt_raw + dtb_ref[...]
    dt = jnp.maximum(u, 0.0) + jnp.log1p(jnp.exp(-jnp.abs(u)))
    dt = jnp.where(row < valid, dt, 0.0)
    a = dt * (-jnp.exp(alog_ref[...]))

    ri = lax.broadcasted_iota(jnp.int32, (L, L), 0)
    ci = lax.broadcasted_iota(jnp.int32, (L, L), 1)
    tri = jnp.where(ri >= ci, 1.0, 0.0).astype(BF16)
    a_cs = _dot01_left(tri, a)

    sel = sel_ref[...]
    acs_b = _dot01_right(a_cs, sel)
    dt_b = _dot01_right(dt, sel)

    li = lax.broadcasted_iota(jnp.int32, (L, width), 0)
    si = lax.broadcasted_iota(jnp.int32, (L, width), 1) & (HEAD_DIM - 1)
    acs_row = jnp.sum(jnp.where(li == si, acs_b, 0.0), axis=0, keepdims=True)
    decay = jnp.where(li >= si, jnp.exp(acs_b - acs_row), 0.0)
    last_b = acs_b[L - 1:L, :]
    e_in = jnp.exp(acs_b)
    e_out = jnp.exp(last_b - acs_b)
    e_last = jnp.exp(last_b)

    xbar = xs * dt_b
    xd16 = (xbar * e_out).astype(BF16)

    pr = lax.broadcasted_iota(jnp.int32, (2 * HEAD_DIM, 2 * HEAD_DIM), 0) >= HEAD_DIM
    pc = lax.broadcasted_iota(jnp.int32, (2 * HEAD_DIM, 2 * HEAD_DIM), 1) >= HEAD_DIM
    pair_mask = pr == pc

    zf = z_ref[...].astype(F32)
    zg = _silu(zf)
    for g in range(SSD_GROUPS):
        lo, hi = g * gw, (g + 1) * gw
        bg = xbc[:, width + g * SSD_STATE:width + (g + 1) * SSD_STATE].astype(BF16)
        cg = xbc[:, width + gn + g * SSD_STATE:width + gn + (g + 1) * SSD_STATE].astype(BF16)
        cb_t = _dot_nt(cg, jnp.concatenate([bg] * hpg, axis=0))
        gmat = (cb_t * decay[:, lo:hi]).astype(BF16)
        st_g = st[:, lo:hi]
        y_off = _dot(cg, st_g.astype(BF16)) * e_in[:, lo:hi]
        parts = []
        for p in range(hpg // 2):
            plo = lo + p * 2 * HEAD_DIM
            xp = xbar[:, plo:plo + 2 * HEAD_DIM]
            bd = jnp.where(pair_mask, jnp.concatenate([xp, xp], axis=0), 0.0).astype(BF16)
            parts.append(_dot(gmat[:, p * 2 * HEAD_DIM:(p + 1) * 2 * HEAD_DIM], bd))
        y_diag = jnp.concatenate(parts, axis=1) if len(parts) > 1 else parts[0]
        y = y_diag + y_off + dskip_ref[:, lo:hi] * xs[:, lo:hi]
        st[:, lo:hi] = st_g * e_last[:, lo:hi] + _dot_tn(bg, xd16[:, lo:hi])

        gt = y[0:valid, :] * zg[:, lo:hi]
        rs = lax.rsqrt(jnp.mean(gt * gt, axis=-1, keepdims=True) + EPS)
        y_ref[:, lo:hi] = (gt * rs * nw_ref[:, lo:hi]).astype(y_ref.dtype)

    @pl.when(c == pl.num_programs(1) - 1)
    def _():
        state_ref[...] = st[...].T


def _ssd(proj, dt_raw, conv_left, init_state, lp, nb, seq, width, cols):
    m = proj.shape[0]
    valid = min(CHUNK, seq)
    nc = seq // valid
    hpg = width // HEAD_DIM // SSD_GROUPS
    gn = SSD_GROUPS * SSD_STATE
    cch = width + 2 * gn
    has_init = init_state is not None
    assert valid >= SSD_CONV - 1 and hpg % 2 == 0 and width // HEAD_DIM <= LANES

    def rows(b, c):
        return b * nc + c

    def col_spec(w, off):
        assert off % w == 0
        return pl.BlockSpec((valid, w), lambda b, c: (rows(b, c), off // w))

    full = lambda shape: pl.BlockSpec(shape, lambda b, c: tuple(0 for _ in shape))
    in_specs = [col_spec(width, cols["z"]), col_spec(width, cols["x"]),
                col_spec(gn, cols["B"]), col_spec(gn, cols["C"]),
                pl.BlockSpec((valid, LANES), lambda b, c: (rows(b, c), 0)),
                pl.BlockSpec((None, SSD_CONV - 1, cch), lambda b, c: (b, 0, 0))]
    args = [proj, proj, proj, proj, dt_raw, conv_left]
    if has_init:
        in_specs.append(pl.BlockSpec((None, width, SSD_STATE), lambda b, c: (b, 0, 0)))
        args.append(init_state)
    in_specs += [full((SSD_CONV, cch)), full((1, cch)), full((1, LANES)), full((1, LANES)),
                 full((1, width)), full((1, width)), full((LANES, width))]
    args += [lp["conv_w"], lp["conv_b"], lp["dt_bias"], lp["a_log"], lp["d_skip"], lp["ssd_norm_w"], lp["sel"]]
    kern = functools.partial(_ssd_kernel, valid=valid, hpg=hpg, has_init=has_init)
    return pl.pallas_call(
        kern,
        grid=(nb, nc),
        in_specs=in_specs,
        out_specs=[pl.BlockSpec((valid, width), lambda b, c: (rows(b, c), 0)),
                   pl.BlockSpec((None, width, SSD_STATE), lambda b, c: (b, 0, 0)),
                   pl.BlockSpec((None, SSD_CONV - 1, cch), lambda b, c: (b, 0, 0))],
        out_shape=[jax.ShapeDtypeStruct((m, width), BF16),
                   jax.ShapeDtypeStruct((nb, width, SSD_STATE), F32),
                   jax.ShapeDtypeStruct((nb, SSD_CONV - 1, cch), F32)],
        scratch_shapes=[pltpu.VMEM((SUBLANES + CHUNK, cch), F32),
                        pltpu.VMEM((SSD_STATE, width), F32)],
        compiler_params=_cparams("arbitrary", "arbitrary"),
        name="ssd",
    )(*args)


def _head_rms(x, w):
    return x * lax.rsqrt(jnp.mean(x * x, axis=-1, keepdims=True) + EPS) * w


def _attn_kernel(sink_ref, q_ref, ka_ref, kb_ref, ko_ref, va_ref, vb_ref, vo_ref, qw_ref, kw_ref,
                 o_ref, kn_ref, *, lq, qpk, prompt):
    c = pl.program_id(1)
    rows = qpk * lq
    nk = WINDOW + lq
    qw = qw_ref[...]
    kw = kw_ref[...]
    scale = HEAD_DIM ** -0.5
    rid = lax.broadcasted_iota(jnp.int32, (rows, 1), 0)
    head_in_group = lax.shift_right_logical(rid, lq.bit_length() - 1)
    l_idx = rid - head_in_group * lq
    s_idx = lax.broadcasted_iota(jnp.int32, (1, nk), 1)
    dist = jnp.abs(WINDOW + l_idx - s_idx).astype(F32)
    if prompt:
        visible = s_idx >= CHUNK * jnp.maximum(WINDOW // CHUNK - c, 0)
    for g in range(KV_HEADS):
        lo, hi = g * HEAD_DIM, (g + 1) * HEAD_DIM
        kon = _head_rms(ko_ref[:, lo:hi].astype(F32), kw)
        kn_ref[:, lo:hi] = kon
        ka = ka_ref[:, lo:hi].astype(F32)
        kb = kb_ref[:, lo:hi].astype(F32)
        if prompt:
            ka = _head_rms(ka, kw)
            kb = _head_rms(kb, kw)
        kmat = jnp.concatenate([ka, kb, kon], axis=0).astype(BF16)
        vmat = jnp.concatenate([va_ref[:, lo:hi].astype(BF16), vb_ref[:, lo:hi].astype(BF16),
                                vo_ref[:, lo:hi].astype(BF16)], axis=0)
        q = jnp.concatenate([q_ref[:, (g * qpk + k) * HEAD_DIM:(g * qpk + k + 1) * HEAD_DIM]
                             for k in range(qpk)], axis=0).astype(F32)
        qn = _head_rms(q, qw).astype(BF16)
        s = _dot_nt(qn, kmat) * scale
        head = (g * qpk + head_in_group + 1).astype(F32)
        slope = jnp.exp2(-8.0 * head / (KV_HEADS * qpk))
        s = s - slope * dist
        if prompt:
            s = jnp.where(visible, s, NEG_INF)
        sink = jnp.zeros((rows, 1), F32)
        for k in range(qpk):
            sink = jnp.where(head_in_group == k, sink_ref[g * qpk + k], sink)
        mx = jnp.maximum(jnp.max(s, axis=-1, keepdims=True), sink)
        p = jnp.exp(s - mx)
        den = jnp.sum(p, axis=-1, keepdims=True) + jnp.exp(sink - mx)
        o = _dot(p.astype(BF16), vmat) / den
        for k in range(0, qpk, 2):
            pair = jnp.concatenate([o[k * lq:(k + 1) * lq, :], o[(k + 1) * lq:(k + 2) * lq, :]], axis=1)
            col = (g * qpk + k) * HEAD_DIM
            o_ref[:, col:col + 2 * HEAD_DIM] = pair.astype(o_ref.dtype)


def _attn(q_src, q_col, k_prev, kp_col, k_own, ko_col, v_prev, vp_col, v_own, vo_col, lp, nb, seq, width, prompt):
    m = nb * seq
    lq = min(CHUNK, seq)
    nc = seq // lq
    heads = width // HEAD_DIM
    qpk = heads // KV_HEADS
    kvw = KV_HEADS * HEAD_DIM
    assert qpk % 2 == 0 and q_col % width == 0 and lq & (lq - 1) == 0
    assert all(cc % kvw == 0 for cc in (kp_col, ko_col, vp_col, vo_col))
    wc = WINDOW // CHUNK
    if prompt:
        def prev(j, col):
            return pl.BlockSpec((CHUNK, kvw), lambda b, c: (b * nc + jnp.maximum(c - wc + j, 0), col // kvw))
    else:
        assert nc == 1
        def prev(j, col):
            return pl.BlockSpec((CHUNK, kvw), lambda b, c: (b * wc + j, col // kvw))
    own = lambda col: pl.BlockSpec((lq, kvw), lambda b, c: (b * nc + c, col // kvw))
    kern = functools.partial(_attn_kernel, lq=lq, qpk=qpk, prompt=prompt)
    return pl.pallas_call(
        kern,
        grid=(nb, nc),
        in_specs=[pl.BlockSpec(memory_space=pltpu.SMEM),
                  pl.BlockSpec((lq, width), lambda b, c: (b * nc + c, q_col // width)),
                  prev(0, kp_col), prev(1, kp_col), own(ko_col),
                  prev(0, vp_col), prev(1, vp_col), own(vo_col),
                  pl.BlockSpec((1, HEAD_DIM), lambda b, c: (0, 0)),
                  pl.BlockSpec((1, HEAD_DIM), lambda b, c: (0, 0))],
        out_specs=[pl.BlockSpec((lq, width), lambda b, c: (b * nc + c, 0)),
                   pl.BlockSpec((lq, kvw), lambda b, c: (b * nc + c, 0))],
        out_shape=[jax.ShapeDtypeStruct((m, width), BF16),
                   jax.ShapeDtypeStruct((m, kvw), F32)],
        compiler_params=_cparams("arbitrary", "arbitrary"),
        name="attn",
    )(lp["sinks"], q_src, k_prev, k_prev, k_own, v_prev, v_prev, v_own, lp["q_norm_w"], lp["k_norm_w"])


def _prep_layer(l, w_ada, b_ada, g_mix, w_in, conv_w, conv_b, dt_bias, a_log, d_skip, ssd_norm_w,
                q_norm_w, k_norm_w, sinks, w_out, g_ffn, w_gate_up, w_down):
    d = w_in.shape[1]
    width = d // 2
    gn = SSD_GROUPS * SSD_STATE
    kvw = KV_HEADS * HEAD_DIM
    heads = width // HEAD_DIM
    ff = w_down.shape[1]
    ffp = _round_up(ff, 1024) if ff > 1024 else _round_up(ff, 256)
    wi = w_in[l]
    o1 = width
    o2 = o1 + width + 2 * gn
    o3 = o2 + heads
    o4 = o3 + width
    o5 = o4 + kvw
    z_w, x_w, b_w, c_w = wi[:, :o1], wi[:, o1:o1 + width], wi[:, o1 + width:o1 + width + gn], wi[:, o1 + width + gn:o2]
    dt_w, q_w, k_w, v_w = wi[:, o2:o3], wi[:, o3:o4], wi[:, o4:o5], wi[:, o5:]
    w_main = jnp.concatenate([z_w, x_w, q_w, b_w, c_w, k_w, v_w], axis=1).astype(BF16)
    cols = {"z": 0, "x": width, "q": 2 * width, "B": 3 * width, "C": 3 * width + gn,
            "k": 3 * width + 2 * gn, "v": 3 * width + 2 * gn + kvw}
    w_dt = jnp.pad(dt_w, ((0, 0), (0, LANES - heads))).astype(BF16)
    wgu = w_gate_up[l]
    pad_ff = ((0, 0), (0, ffp - ff))
    w_gu = jnp.concatenate([jnp.pad(wgu[:, :ff], pad_ff), jnp.pad(wgu[:, ff:], pad_ff)], axis=1).astype(BF16)
    w_dn = jnp.pad(w_down[l], ((0, ffp - ff), (0, 0))).astype(BF16)
    head_of_lane = jnp.arange(width) // HEAD_DIM
    sel = (jnp.arange(LANES)[:, None] == head_of_lane[None, :]).astype(BF16)
    pad_h = (0, LANES - heads)
    return dict(
        w_ada=w_ada[l], b_ada=b_ada[l], g_mix=g_mix[l], g_ffn=g_ffn[l],
        w_main=w_main, w_dt=w_dt, cols=cols, w_out=w_out[l].astype(BF16), w_gu=w_gu, w_dn=w_dn, ffp=ffp,
        conv_w=conv_w[l], conv_b=conv_b[l].reshape(1, -1),
        dt_bias=jnp.pad(dt_bias[l], pad_h).reshape(1, LANES), a_log=jnp.pad(a_log[l], pad_h).reshape(1, LANES),
        d_skip=jnp.repeat(d_skip[l], HEAD_DIM).reshape(1, width), ssd_norm_w=ssd_norm_w[l].reshape(1, width),
        sel=sel, q_norm_w=q_norm_w[l].reshape(1, HEAD_DIM), k_norm_w=k_norm_w[l].reshape(1, HEAD_DIM),
        sinks=sinks[l], width=width)


def _layer(x, mods, boff, lp, ssd_init, conv_left, cache_kv):
    nb, seq, d = x.shape
    m = nb * seq
    width = lp["width"]
    cols = lp["cols"]
    kvw = KV_HEADS * HEAD_DIM
    sh1, sc1, g1, sh2, sc2, g2 = mods
    x2 = x.reshape(m, d)
    h = _norm_mod(x2, lp["g_mix"], sc1, sh1, seq, boff)
    proj = _mm(h, lp["w_main"], BF16, (768, 512, 256, 128))
    dt_raw = _mm(h, lp["w_dt"], F32, (LANES,))
    y, ssd_state, conv_state = _ssd(proj, dt_raw, conv_left, ssd_init, lp, nb, seq, width, cols)
    if cache_kv is None:
        o, kn = _attn(proj, cols["q"], proj, cols["k"], proj, cols["k"], proj, cols["v"], proj, cols["v"],
                      lp, nb, seq, width, True)
    else:
        ck = cache_kv[0].reshape(nb * WINDOW, kvw)
        cv = cache_kv[1].reshape(nb * WINDOW, kvw)
        o, kn = _attn(proj, cols["q"], ck, 0, proj, cols["k"], cv, 0, proj, cols["v"], lp, nb, seq, width, False)
    x1 = _mm_out(y, o, lp["w_out"], x2, g1, seq, boff)
    h2 = _norm_mod(x1, lp["g_ffn"], sc2, sh2, seq, boff)
    act = _mm_gate_up(h2, lp["w_gu"], lp["ffp"])
    out = _mm_down(act, lp["w_dn"], x1, g2, seq, boff)
    keep = min(WINDOW, seq) if cache_kv is None else seq
    heads = width // HEAD_DIM
    k_state = kn.reshape(nb, seq, KV_HEADS, HEAD_DIM)[:, seq - keep:]
    v_state = proj[:, cols["v"]:cols["v"] + kvw].astype(F32).reshape(nb, seq, KV_HEADS, HEAD_DIM)[:, seq - keep:]
    ssd_state = ssd_state.reshape(nb, heads, HEAD_DIM, SSD_STATE)
    return out.reshape(nb, seq, d), ssd_state, conv_state, k_state, v_state


def kernel(x_prompt, x_sample, state_ssd, state_conv, cache_k, cache_v, c_prompt, c_sample, w_ada, b_ada, g_mix, w_in, conv_w, conv_b, dt_bias, a_log, d_skip, ssd_norm_w, q_norm_w, k_norm_w, sinks, w_out, g_ffn, w_gate_up, w_down):
    depth = w_in.shape[0]
    bp, _, d = x_prompt.shape
    bs = x_sample.shape[0]
    width = d // 2
    assert cache_k.shape[2] == WINDOW and x_sample.shape[1] <= CHUNK
    yp, ys = x_prompt, x_sample
    outs = [[] for _ in range(8)]
    c_all = jnp.concatenate([c_prompt, c_sample], axis=0)
    zero_conv = jnp.zeros((bp, SSD_CONV - 1, width + 2 * SSD_GROUPS * SSD_STATE), F32)
    for l in range(depth):
        lp = _prep_layer(l, w_ada, b_ada, g_mix, w_in, conv_w, conv_b, dt_bias, a_log, d_skip, ssd_norm_w,
                         q_norm_w, k_norm_w, sinks, w_out, g_ffn, w_gate_up, w_down)
        mod = _ada(c_all, lp["w_ada"], lp["b_ada"]).reshape(bp + bs, N_MOD, 1, d)
        mods = [mod[:, i] for i in range(N_MOD)]
        yp, s1, s2, s3, s4 = _layer(yp, mods, 0, lp, None, zero_conv, None)
        init = state_ssd[l].reshape(bs, width, SSD_STATE)
        ys, t1, t2, t3, t4 = _layer(ys, mods, bp, lp, init, state_conv[l], (cache_k[l], cache_v[l]))
        for lst, v in zip(outs, (s1, s2, s3, s4, t1, t2, t3, t4)):
            lst.append(v)
    stacked = [jnp.stack(v, axis=0) for v in outs]
    return (yp, ys, *stacked)
```

```python
import functools
import math

import numpy as np
import jax
import jax.numpy as jnp
from jax import lax
from jax.experimental import pallas as pl
from jax.experimental.pallas import tpu as pltpu

CHUNK = 64
HEAD_DIM = 64
SSD_GROUPS = 4
SSD_STATE = 128
SSD_CONV = 4
KV_HEADS = 4
WINDOW = 128
N_MOD = 6
EPS = 1e-6
LOG2E = math.log2(math.e)

LANES = 128
SUBLANES = 8
BF16_ROWS = 16
MXU_WIDTH = 256
VMEM_LIMIT_BYTES = 56 * 1024 * 1024

PACK_STRIDE = 32

F32 = jnp.float32
BF16 = jnp.bfloat16


def _cparams(*sem):
    return pltpu.CompilerParams(dimension_semantics=sem, vmem_limit_bytes=VMEM_LIMIT_BYTES)


def _pick(n, candidates):
    for c in candidates:
        if c <= n and n % c == 0:
            return c
    return n


def _silu(x):
    h = 0.5 * x
    return h * jnp.tanh(h) + h


def _dot(a, b):
    return jnp.dot(a, b, preferred_element_type=F32)


def _dot_nt(a, b):
    return lax.dot_general(a, b, (((1,), (1,)), ((), ())), preferred_element_type=F32)


def _dot_tn(a, b):
    return lax.dot_general(a, b, (((0,), (0,)), ((), ())), preferred_element_type=F32)


def _split3(v):
    hi = v.astype(BF16)
    r1 = v - hi.astype(F32)
    mid = r1.astype(BF16)
    lo = (r1 - mid.astype(F32)).astype(BF16)
    return hi, mid, lo


def _dot01_left(m01, v):
    hi, mid, lo = _split3(v)
    return _dot(m01, hi) + (_dot(m01, mid) + _dot(m01, lo))


def _pack3(v):
    hi, mid, lo = _split3(v)
    packed = hi.astype(F32) + pltpu.roll(mid.astype(F32), PACK_STRIDE, 1) + pltpu.roll(lo.astype(F32), 2 * PACK_STRIDE, 1)
    return packed.astype(BF16)


def _ada_kernel(c_ref, w_ref, b_ref, o_ref):
    a = _silu(c_ref[...]).astype(BF16)
    o_ref[...] = _dot(a, w_ref[...].astype(BF16)) + b_ref[...]


def _ada(c, w, b):
    r, d = c.shape
    n = w.shape[1]
    tn = _pick(n, (512, 256, 128))
    return pl.pallas_call(
        _ada_kernel,
        grid=(n // tn,),
        in_specs=[pl.BlockSpec((r, d), lambda j: (0, 0)),
                  pl.BlockSpec((d, tn), lambda j: (0, j)),
                  pl.BlockSpec((1, tn), lambda j: (0, j))],
        out_specs=pl.BlockSpec((r, tn), lambda j: (0, j)),
        out_shape=jax.ShapeDtypeStruct((r, n), F32),
        compiler_params=_cparams("arbitrary"),
        name="ada",
    )(c, w, b.reshape(1, n))


def _norm_kernel(x_ref, g_ref, sc_ref, sh_ref, o_ref):
    x = x_ref[...]
    r = lax.rsqrt(jnp.mean(x * x, axis=-1, keepdims=True) + EPS)
    y = x * r * g_ref[...]
    o_ref[...] = (y * (1.0 + sc_ref[...]) + sh_ref[...]).astype(o_ref.dtype)


def _norm_mod(x, g, sc, sh, seq, boff):
    m, d = x.shape
    tm = _pick(seq, (256, 128, 64, 32, 16, 8))
    per = seq // tm
    mod_spec = pl.BlockSpec((None, 1, d), lambda i: (boff + i // per, 0, 0))
    return pl.pallas_call(
        _norm_kernel,
        grid=(m // tm,),
        in_specs=[pl.BlockSpec((tm, d), lambda i: (i, 0)),
                  pl.BlockSpec((1, d), lambda i: (0, 0)),
                  mod_spec, mod_spec],
        out_specs=pl.BlockSpec((tm, d), lambda i: (i, 0)),
        out_shape=jax.ShapeDtypeStruct((m, d), BF16),
        compiler_params=_cparams("arbitrary"),
        name="norm_mod",
    )(x, g.reshape(1, d), sc, sh)


def _mm_in_kernel(a_ref, w_ref, wdt_ref, o_ref, dt_ref):
    a = a_ref[...]
    o_ref[...] = _dot(a, w_ref[...]).astype(o_ref.dtype)

    @pl.when(pl.program_id(1) == 0)
    def _():
        dt_ref[...] = _dot(a, wdt_ref[...])


def _mm_in(a, w, w_dt):
    m, k = a.shape
    n = w.shape[1]
    tm = _pick(m, (1024, 512, 256, 128))
    tn = _pick(n, (768, 512, 256, 128))
    return pl.pallas_call(
        _mm_in_kernel,
        grid=(m // tm, n // tn),
        in_specs=[pl.BlockSpec((tm, k), lambda i, j: (i, 0)),
                  pl.BlockSpec((k, tn), lambda i, j: (0, j)),
                  pl.BlockSpec((k, LANES), lambda i, j: (0, 0))],
        out_specs=[pl.BlockSpec((tm, tn), lambda i, j: (i, j)),
                   pl.BlockSpec((tm, LANES), lambda i, j: (i, 0))],
        out_shape=[jax.ShapeDtypeStruct((m, n), BF16),
                   jax.ShapeDtypeStruct((m, LANES), F32)],
        compiler_params=_cparams("arbitrary", "arbitrary"),
        name="mm_in",
    )(a, w, w_dt)


def _gate_operand(gate, tm, tn, seq, boff, nb):
    if seq % tm == 0:
        per = seq // tm
        return gate, pl.BlockSpec((None, 1, tn), lambda i, j: (boff + i // per, 0, j))
    rows = jnp.repeat(gate[boff:boff + nb, 0], seq, axis=0)
    return rows, pl.BlockSpec((tm, tn), lambda i, j: (i, j))


def _mm_out_kernel(y_ref, o_ref, wa_ref, wb_ref, x_ref, g_ref, out_ref):
    acc = _dot(y_ref[...], wa_ref[...]) + _dot(o_ref[...], wb_ref[...])
    out_ref[...] = x_ref[...] + g_ref[...] * acc


def _mm_out(y, o, w, x, gate, seq, boff):
    m, wd = y.shape
    d = w.shape[1]
    tm = _pick(m, (1024, 512, 256, 128))
    tn = _pick(d, (512, 256, 128))
    gate, gate_spec = _gate_operand(gate, tm, tn, seq, boff, m // seq)
    return pl.pallas_call(
        _mm_out_kernel,
        grid=(m // tm, d // tn),
        in_specs=[pl.BlockSpec((tm, wd), lambda i, j: (i, 0)),
                  pl.BlockSpec((tm, wd), lambda i, j: (i, 0)),
                  pl.BlockSpec((wd, tn), lambda i, j: (0, j)),
                  pl.BlockSpec((wd, tn), lambda i, j: (1, j)),
                  pl.BlockSpec((tm, tn), lambda i, j: (i, j)),
                  gate_spec],
        out_specs=pl.BlockSpec((tm, tn), lambda i, j: (i, j)),
        out_shape=jax.ShapeDtypeStruct((m, d), F32),
        compiler_params=_cparams("arbitrary", "arbitrary"),
        name="mm_out",
    )(y, o, w, w, x, gate)


def _mm_gu_kernel(h_ref, wg_ref, wu_ref, o_ref):
    h = h_ref[...]
    g = _dot(h, wg_ref[...])
    u = _dot(h, wu_ref[...])
    o_ref[...] = (_silu(g) * u).astype(o_ref.dtype)


def _mm_gate_up(h, w):
    m, d = h.shape
    ff = w.shape[1] // 2
    tm = _pick(m, (1024, 512, 256, 128))
    tn = _pick(ff, (512, 256, 128))
    nj = ff // tn
    return pl.pallas_call(
        _mm_gu_kernel,
        grid=(m // tm, nj),
        in_specs=[pl.BlockSpec((tm, d), lambda i, j: (i, 0)),
                  pl.BlockSpec((d, tn), lambda i, j: (0, j)),
                  pl.BlockSpec((d, tn), lambda i, j: (0, nj + j))],
        out_specs=pl.BlockSpec((tm, tn), lambda i, j: (i, j)),
        out_shape=jax.ShapeDtypeStruct((m, ff), BF16),
        compiler_params=_cparams("arbitrary", "arbitrary"),
        name="mm_gate_up",
    )(h, w, w)


def _mm_down_kernel(a_ref, w_ref, x_ref, g_ref, o_ref):
    o_ref[...] = x_ref[...] + g_ref[...] * _dot(a_ref[...], w_ref[...])


def _mm_down(a, w, x, gate, seq, boff):
    m, f = a.shape
    d = w.shape[1]
    tm = _pick(m, (512, 256, 128))
    tn = _pick(d, (512, 256, 128))
    gate, gate_spec = _gate_operand(gate, tm, tn, seq, boff, m // seq)
    return pl.pallas_call(
        _mm_down_kernel,
        grid=(m // tm, d // tn),
        in_specs=[pl.BlockSpec((tm, f), lambda i, j: (i, 0)),
                  pl.BlockSpec((f, tn), lambda i, j: (0, j)),
                  pl.BlockSpec((tm, tn), lambda i, j: (i, j)),
                  gate_spec],
        out_specs=pl.BlockSpec((tm, tn), lambda i, j: (i, j)),
        out_shape=jax.ShapeDtypeStruct((m, d), F32),
        compiler_params=_cparams("arbitrary", "arbitrary"),
        name="mm_down",
    )(a, w, x, gate)


LEFT_ROWS = 3 * BF16_ROWS
SHIFT_ROWS = SSD_CONV * CHUNK + SUBLANES


def _shift_matrix(valid):
    keep = SSD_CONV - 1
    t = np.zeros((SHIFT_ROWS, LEFT_ROWS + valid), np.float32)
    for k in range(SSD_CONV):
        for r in range(valid):
            src = r + k - keep
            if src >= 0:
                t[k * CHUNK + r, LEFT_ROWS + src] = 1.0
            else:
                for piece in range(3):
                    t[k * CHUNK + r, piece * BF16_ROWS + BF16_ROWS + src] = 1.0
    for r in range(keep):
        t[SSD_CONV * CHUNK + r, LEFT_ROWS + valid - keep + r] = 1.0
    return jnp.asarray(t, BF16)


def _ssd_kernel(*refs, valid, hpg, has_init):
    if has_init:
        (z_ref, x_ref, b_ref, c_ref, dt_ref, convl_ref, init_ref, tmat_ref, cw_ref, cb_ref, dtb_ref, alog_ref,
         dskip_ref, nw_ref, sel_ref, y_ref, state_ref, convs_ref, left, tmp, st) = refs
    else:
        (z_ref, x_ref, b_ref, c_ref, dt_ref, convl_ref, tmat_ref, cw_ref, cb_ref, dtb_ref, alog_ref,
         dskip_ref, nw_ref, sel_ref, y_ref, state_ref, convs_ref, left, tmp, st) = refs
        init_ref = None
    c = pl.program_id(1)
    width = x_ref.shape[1]
    heads = width // HEAD_DIM
    gn = SSD_GROUPS * SSD_STATE
    gw = hpg * HEAD_DIM
    L = CHUNK
    keep = SSD_CONV - 1

    @pl.when(c == 0)
    def _():
        tmp[...] = jnp.zeros_like(tmp)
        tmp[BF16_ROWS - keep:BF16_ROWS, :] = convl_ref[...]
        hi, mid, lo = _split3(tmp[...])
        left[0:BF16_ROWS, :] = lo
        left[BF16_ROWS:2 * BF16_ROWS, :] = mid
        left[2 * BF16_ROWS:LEFT_ROWS, :] = hi
        if has_init:
            st[...] = init_ref[...].T
        else:
            st[...] = jnp.zeros_like(st)

    tmat = tmat_ref[...]

    def conv(ref, lo, hi):
        sh = _dot(tmat, jnp.concatenate([left[:, lo:hi], ref[...]], axis=0))
        acc = cb_ref[:, lo:hi] + sh[0:L] * cw_ref[0:1, lo:hi]
        for k in range(1, SSD_CONV):
            acc = acc + sh[k * L:(k + 1) * L] * cw_ref[k:k + 1, lo:hi]
        convs_ref[:, lo:hi] = sh[SSD_CONV * L:SSD_CONV * L + keep]
        return _silu(acc)

    xs = conv(x_ref, 0, width)
    bc = conv(b_ref, width, width + gn)
    cc = conv(c_ref, width + gn, width + 2 * gn)

    left[0:2 * BF16_ROWS, :] = jnp.zeros((2 * BF16_ROWS, left.shape[1]), BF16)
    left[2 * BF16_ROWS:LEFT_ROWS, 0:width] = x_ref[valid - BF16_ROWS:valid, :]
    left[2 * BF16_ROWS:LEFT_ROWS, width:width + gn] = b_ref[valid - BF16_ROWS:valid, :]
    left[2 * BF16_ROWS:LEFT_ROWS, width + gn:width + 2 * gn] = c_ref[valid - BF16_ROWS:valid, :]

    row = lax.broadcasted_iota(jnp.int32, (L, LANES), 0)
    lane = lax.broadcasted_iota(jnp.int32, (L, LANES), 1)
    if valid < L:
        dt_raw = jnp.concatenate([dt_ref[...], jnp.zeros((L - valid, LANES), F32)], axis=0)
    else:
        dt_raw = dt_ref[...]
    u = dt_raw + dtb_ref[...]
    dt = jnp.maximum(u, 0.0) + jnp.log1p(jnp.exp(-jnp.abs(u)))
    dt = jnp.where((row < valid) & (lane < heads), dt, 0.0)
    a = dt * (-jnp.exp(alog_ref[...]))

    tri = jnp.where(lax.broadcasted_iota(jnp.int32, (L, L), 0) >= lax.broadcasted_iota(jnp.int32, (L, L), 1),
                    1.0, 0.0).astype(BF16)
    a_cs = _dot01_left(tri, a)
    both = _dot(jnp.concatenate([_pack3(a_cs), _pack3(dt)], axis=0), sel_ref[...])
    acs_b = both[0:L]
    dt_b = both[L:2 * L]

    s_idx = lane & (HEAD_DIM - 1)
    causal = jnp.where(row >= s_idx, 0.0, -jnp.inf)
    diag = row == s_idx
    dec = []
    for j in range(width // LANES):
        blk = acs_b[:, j * LANES:(j + 1) * LANES]
        acs_row = jnp.sum(jnp.where(diag, blk, 0.0), axis=0, keepdims=True)
        dec.append(jnp.exp(blk - acs_row + causal))
    decay = jnp.concatenate(dec, axis=1)
    last_b = acs_b[L - 1:L, :]
    e_in = jnp.exp(acs_b)
    e_out = jnp.exp(last_b - acs_b)
    e_last = jnp.exp(last_b)

    xbar = xs * dt_b
    xd16 = (xbar * e_out).astype(BF16)

    pr = lax.broadcasted_iota(jnp.int32, (2 * HEAD_DIM, 2 * HEAD_DIM), 0) >= HEAD_DIM
    pc = lax.broadcasted_iota(jnp.int32, (2 * HEAD_DIM, 2 * HEAD_DIM), 1) >= HEAD_DIM
    pair_mask = pr == pc

    zg = _silu(z_ref[...].astype(F32))
    groups = range(SSD_GROUPS)
    span = lambda g: slice(g * gw, (g + 1) * gw)
    bgs = [bc[:, g * SSD_STATE:(g + 1) * SSD_STATE].astype(BF16) for g in groups]
    cgs = [cc[:, g * SSD_STATE:(g + 1) * SSD_STATE].astype(BF16) for g in groups]
    cb_ts = [_dot_nt(cgs[g], jnp.concatenate([bgs[g]] * hpg, axis=0)) for g in groups]
    st_gs = [st[:, span(g)] for g in groups]
    y_offs = [_dot(cgs[g], st_gs[g].astype(BF16)) for g in groups]
    st_new = [_dot_tn(bgs[g], xd16[:, span(g)]) for g in groups]
    for g in groups:
        st[:, span(g)] = st_gs[g] * e_last[:, span(g)] + st_new[g]
    y_diags = []
    for g in groups:
        gmat = (cb_ts[g] * decay[:, span(g)]).astype(BF16)
        for p in range(hpg // 2):
            plo = g * gw + p * 2 * HEAD_DIM
            xp = xbar[:, plo:plo + 2 * HEAD_DIM]
            bd = jnp.where(pair_mask, jnp.concatenate([xp, xp], axis=0), 0.0).astype(BF16)
            y_diags.append(_dot(gmat[:, p * 2 * HEAD_DIM:(p + 1) * 2 * HEAD_DIM], bd))
    for g in groups:
        y_diag = jnp.concatenate(y_diags[g * (hpg // 2):(g + 1) * (hpg // 2)], axis=1)
        y = y_diag + y_offs[g] * e_in[:, span(g)] + dskip_ref[:, span(g)] * xs[:, span(g)]
        gt = y[0:valid, :] * zg[:, span(g)]
        rs = lax.rsqrt(jnp.mean(gt * gt, axis=-1, keepdims=True) + EPS)
        y_ref[:, span(g)] = (gt * rs * nw_ref[:, span(g)]).astype(y_ref.dtype)

    @pl.when(c == pl.num_programs(1) - 1)
    def _():
        state_ref[...] = st[...].T


def _ssd(proj, dt_raw, conv_left, init_state, lp, nb, seq, width, cols):
    m = proj.shape[0]
    valid = min(CHUNK, seq)
    nc = seq // valid
    heads = width // HEAD_DIM
    hpg = heads // SSD_GROUPS
    gn = SSD_GROUPS * SSD_STATE
    cch = width + 2 * gn
    has_init = init_state is not None
    assert valid % BF16_ROWS == 0 and hpg % 2 == 0 and heads <= PACK_STRIDE and 3 * PACK_STRIDE <= LANES

    def rows(b, c):
        return b * nc + c

    def col_spec(w, off):
        assert off % w == 0
        return pl.BlockSpec((valid, w), lambda b, c: (rows(b, c), off // w))

    full = lambda shape: pl.BlockSpec(shape, lambda b, c: tuple(0 for _ in shape))
    in_specs = [col_spec(width, cols["z"]), col_spec(width, cols["x"]),
                col_spec(gn, cols["B"]), col_spec(gn, cols["C"]),
                pl.BlockSpec((valid, LANES), lambda b, c: (rows(b, c), 0)),
                pl.BlockSpec((None, SSD_CONV - 1, cch), lambda b, c: (b, 0, 0))]
    args = [proj, proj, proj, proj, dt_raw, conv_left]
    if has_init:
        in_specs.append(pl.BlockSpec((None, width, SSD_STATE), lambda b, c: (b, 0, 0)))
        args.append(init_state)
    tmat = _shift_matrix(valid)
    in_specs += [full(tmat.shape), full((SSD_CONV, cch)), full((1, cch)), full((1, LANES)), full((1, LANES)),
                 full((1, width)), full((1, width)), full((LANES, width))]
    args += [tmat, lp["conv_w"], lp["conv_b"], lp["dt_bias"], lp["a_log"], lp["d_skip"], lp["ssd_norm_w"], lp["sel"]]
    kern = functools.partial(_ssd_kernel, valid=valid, hpg=hpg, has_init=has_init)
    return pl.pallas_call(
        kern,
        grid=(nb, nc),
        in_specs=in_specs,
        out_specs=[pl.BlockSpec((valid, width), lambda b, c: (rows(b, c), 0)),
                   pl.BlockSpec((None, width, SSD_STATE), lambda b, c: (b, 0, 0)),
                   pl.BlockSpec((None, SSD_CONV - 1, cch), lambda b, c: (b, 0, 0))],
        out_shape=[jax.ShapeDtypeStruct((m, width), BF16),
                   jax.ShapeDtypeStruct((nb, width, SSD_STATE), F32),
                   jax.ShapeDtypeStruct((nb, SSD_CONV - 1, cch), F32)],
        scratch_shapes=[pltpu.VMEM((LEFT_ROWS, cch), BF16),
                        pltpu.VMEM((BF16_ROWS, cch), F32),
                        pltpu.VMEM((SSD_STATE, width), F32)],
        compiler_params=_cparams("arbitrary", "arbitrary"),
        name="ssd",
    )(*args)


def _head_rms(x, w):
    return x * lax.rsqrt(jnp.mean(x * x, axis=-1, keepdims=True) + EPS) * w


def _attn_kernel(*refs, lq, qpk, hb, prompt):
    if prompt:
        q_ref, ko_ref, vo_ref = refs[:3]
        rest = refs[3:]
    else:
        q_ref, ko_ref, vo_ref, kp_ref, vp_ref = refs[:5]
        rest = refs[5:]
    kw_ref, ks_ref, sink_ref, dup_ref, eye_ref, mq_ref, mo_ref, ep_ref, o_ref, kn_ref, kring, vring, bias = rest
    b = pl.program_id(0)
    c = pl.program_id(1)
    nk = WINDOW + lq
    heads = KV_HEADS * qpk
    wc = WINDOW // CHUNK

    @pl.when((b == 0) & (c == 0))
    def _():
        j = lax.broadcasted_iota(jnp.int32, (nk, heads * lq), 1)
        s = lax.broadcasted_iota(jnp.int32, (nk, heads * lq), 0)
        h = lax.shift_right_logical(j, lq.bit_length() - 1)
        l_idx = j & (lq - 1)
        slope = jnp.exp2(-8.0 * (h + 1).astype(F32) / heads)
        bias[...] = -(slope * LOG2E) * jnp.abs(WINDOW + l_idx - s).astype(F32)

    if prompt:
        @pl.when(c == 0)
        def _():
            kring[...] = jnp.zeros_like(kring)
            vring[...] = jnp.zeros_like(vring)
    else:
        kring[0:WINDOW, :] = (kp_ref[...] * ks_ref[...]).astype(BF16)
        vring[0:WINDOW, :] = vp_ref[...].astype(BF16)

    kw = kw_ref[...]
    for g in range(KV_HEADS):
        lo, hi = g * HEAD_DIM, (g + 1) * HEAD_DIM
        kn_ref[:, lo:hi] = _head_rms(ko_ref[:, lo:hi].astype(F32), kw)
    kring[WINDOW:nk, :] = (kn_ref[...] * ks_ref[...]).astype(BF16)
    vring[WINDOW:nk, :] = vo_ref[...].astype(BF16)

    hw = hb * HEAD_DIM
    pw = hb * lq
    kdup = _dot(kring[...], dup_ref[...]).astype(BF16)
    vt = _dot_nt(eye_ref[...], vring[...]).astype(BF16)
    mask_q = mq_ref[...]
    mask_o = mo_ref[...]
    eye_blk = ep_ref[...]
    ones = jnp.ones((SUBLANES, hw), BF16)
    if prompt:
        offs = [jnp.where(c >= wc - i, 0.0, jnp.inf) for i in range(wc)]
    pairs = range(heads // hb)
    group = lambda pi: pi // (qpk // hb)
    qbs = []
    for pi in pairs:
        qp = q_ref[:, pi * hw:(pi + 1) * hw]
        qbs.append(jnp.concatenate([qp] * hb, axis=0) * mask_q)
    ssqs = [_dot_nt(ones, qb * qb)[0:1] for qb in qbs]
    raw = [_dot_nt(kdup[:, group(pi) * hw:(group(pi) + 1) * hw], qbs[pi]) for pi in pairs]
    pns = []
    for pi in pairs:
        rq = lax.rsqrt(ssqs[pi] * (1.0 / HEAD_DIM) + EPS)
        s2 = raw[pi] * rq + bias[:, pi * pw:(pi + 1) * pw]
        sink2 = sink_ref[:, pi * pw:(pi + 1) * pw]
        m_own = jnp.maximum(jnp.max(s2[WINDOW:nk], axis=0, keepdims=True), sink2)
        if prompt:
            mx = m_own
            for i in range(wc):
                band = s2[i * CHUNK:(i + 1) * CHUNK]
                mx = jnp.maximum(mx, jnp.max(band, axis=0, keepdims=True) - offs[i])
            p = jnp.concatenate([jnp.exp2(s2[i * CHUNK:(i + 1) * CHUNK] - (mx + offs[i])) for i in range(wc)]
                                + [jnp.exp2(s2[WINDOW:nk] - mx)], axis=0)
        else:
            mx = jnp.maximum(m_own, jnp.max(s2[0:WINDOW], axis=0, keepdims=True))
            p = jnp.exp2(s2 - mx)
        den = jnp.sum(p, axis=0, keepdims=True) + jnp.exp2(sink2 - mx)
        pns.append((p * (1.0 / den)).astype(BF16))
    ots = [_dot(vt[group(pi) * HEAD_DIM:(group(pi) + 1) * HEAD_DIM, :], pns[pi]).astype(BF16)
           for pi in pairs]
    obs = [jnp.concatenate([ot] * hb, axis=0) * mask_o for ot in ots]
    outs = [_dot_nt(eye_blk, ob) for ob in obs]
    for pi in pairs:
        o_ref[:, pi * hw:(pi + 1) * hw] = outs[pi].astype(o_ref.dtype)

    if prompt:
        kring[0:WINDOW, :] = kring[lq:nk, :]
        vring[0:WINDOW, :] = vring[lq:nk, :]


def _attn(proj, cols, cache_kv, lp, nb, seq, width):
    m = nb * seq
    lq = min(CHUNK, seq)
    nc = seq // lq
    heads = width // HEAD_DIM
    qpk = heads // KV_HEADS
    kvw = KV_HEADS * HEAD_DIM
    prompt = cache_kv is None
    nk = WINDOW + lq
    assert qpk % 2 == 0 and lq & (lq - 1) == 0 and lq % BF16_ROWS == 0 and WINDOW % CHUNK == 0
    assert cols["q"] % width == 0 and cols["k"] % kvw == 0 and cols["v"] % kvw == 0
    assert prompt and lq == CHUNK or not prompt and nc == 1
    row_spec = lambda w, off: pl.BlockSpec((lq, w), lambda b, c: (b * nc + c, off // w))
    full = lambda shape: pl.BlockSpec(shape, lambda b, c: tuple(0 for _ in shape))
    in_specs = [row_spec(width, cols["q"]), row_spec(kvw, cols["k"]), row_spec(kvw, cols["v"])]
    args = [proj, proj, proj]
    if not prompt:
        in_specs += [pl.BlockSpec((WINDOW, kvw), lambda b, c: (b, 0))] * 2
        args += list(cache_kv)
    sink2 = (jnp.repeat(lp["sinks"], lq) * LOG2E).reshape(1, heads * lq)
    hb = MXU_WIDTH // HEAD_DIM if qpk % (MXU_WIDTH // HEAD_DIM) == 0 else 2
    hw = hb * HEAD_DIM
    kv_idx = np.arange(kvw)
    dup_idx = np.arange(KV_HEADS * hw)
    dup = (kv_idx[:, None] // HEAD_DIM == dup_idx[None, :] // hw) & (kv_idx[:, None] % HEAD_DIM == dup_idx[None, :] % HEAD_DIM)
    ql = np.arange(hb * lq)
    dl = np.arange(hw)
    mask_q = ql[:, None] // lq == dl[None, :] // HEAD_DIM
    eye_blk = np.arange(lq)[:, None] == ql[None, :] % lq
    consts = [jnp.asarray(a, BF16) for a in (dup, np.eye(kvw), mask_q, mask_q.T, eye_blk)]
    in_specs += [full((1, HEAD_DIM)), full((1, kvw)), full((1, heads * lq))] + [full(a.shape) for a in consts]
    args += [lp["k_norm_w"], lp["k_scale"], sink2] + consts
    kern = functools.partial(_attn_kernel, lq=lq, qpk=qpk, hb=hb, prompt=prompt)
    return pl.pallas_call(
        kern,
        grid=(nb, nc),
        in_specs=in_specs,
        out_specs=[pl.BlockSpec((lq, width), lambda b, c: (b * nc + c, 0)),
                   pl.BlockSpec((lq, kvw), lambda b, c: (b * nc + c, 0))],
        out_shape=[jax.ShapeDtypeStruct((m, width), BF16),
                   jax.ShapeDtypeStruct((m, kvw), F32)],
        scratch_shapes=[pltpu.VMEM((nk, kvw), BF16),
                        pltpu.VMEM((nk, kvw), BF16),
                        pltpu.VMEM((nk, heads * lq), F32)],
        compiler_params=_cparams("arbitrary", "arbitrary"),
        name="attn",
    )(*args)


def _prep_layer(l, w_ada, b_ada, g_mix, w_in, conv_w, conv_b, dt_bias, a_log, d_skip, ssd_norm_w,
                q_norm_w, k_norm_w, sinks, w_out, g_ffn, w_gate_up, w_down):
    d = w_in.shape[1]
    width = d // 2
    gn = SSD_GROUPS * SSD_STATE
    kvw = KV_HEADS * HEAD_DIM
    heads = width // HEAD_DIM
    wi = w_in[l]
    o1 = width
    o2 = o1 + width + 2 * gn
    o3 = o2 + heads
    o4 = o3 + width
    o5 = o4 + kvw
    z_w, x_w, b_w, c_w = wi[:, :o1], wi[:, o1:o1 + width], wi[:, o1 + width:o1 + width + gn], wi[:, o1 + width + gn:o2]
    dt_w, q_w, k_w, v_w = wi[:, o2:o3], wi[:, o3:o4], wi[:, o4:o5], wi[:, o5:]
    w_main = jnp.concatenate([z_w, x_w, q_w, b_w, c_w, k_w, v_w], axis=1).astype(BF16)
    cols = {"z": 0, "x": width, "q": 2 * width, "B": 3 * width, "C": 3 * width + gn,
            "k": 3 * width + 2 * gn, "v": 3 * width + 2 * gn + kvw}
    w_dt = jnp.pad(dt_w, ((0, 0), (0, LANES - heads))).astype(BF16)
    lane_piece = jnp.arange(LANES)
    head_of_lane = jnp.arange(width) // HEAD_DIM
    sel = ((lane_piece[:, None] % PACK_STRIDE == head_of_lane[None, :])
           & (lane_piece[:, None] < 3 * PACK_STRIDE)).astype(BF16)
    pad_h = (0, LANES - heads)
    k_scale = jnp.tile(q_norm_w[l], KV_HEADS) * (HEAD_DIM ** -0.5 * LOG2E)
    return dict(
        w_ada=w_ada[l], b_ada=b_ada[l], g_mix=g_mix[l], g_ffn=g_ffn[l],
        w_main=w_main, w_dt=w_dt, cols=cols, w_out=w_out[l].astype(BF16),
        w_gu=w_gate_up[l].astype(BF16), w_dn=w_down[l].astype(BF16),
        conv_w=conv_w[l], conv_b=conv_b[l].reshape(1, -1),
        dt_bias=jnp.pad(dt_bias[l], pad_h).reshape(1, LANES), a_log=jnp.pad(a_log[l], pad_h).reshape(1, LANES),
        d_skip=jnp.repeat(d_skip[l], HEAD_DIM).reshape(1, width), ssd_norm_w=ssd_norm_w[l].reshape(1, width),
        sel=sel, k_norm_w=k_norm_w[l].reshape(1, HEAD_DIM), k_scale=k_scale.reshape(1, kvw),
        sinks=sinks[l], width=width)


def _layer(x, mods, boff, lp, ssd_init, conv_left, cache_kv):
    nb, seq, d = x.shape
    m = nb * seq
    width = lp["width"]
    cols = lp["cols"]
    kvw = KV_HEADS * HEAD_DIM
    sh1, sc1, g1, sh2, sc2, g2 = mods
    x2 = x.reshape(m, d)
    h = _norm_mod(x2, lp["g_mix"], sc1, sh1, seq, boff)
    proj, dt_raw = _mm_in(h, lp["w_main"], lp["w_dt"])
    y, ssd_state, conv_state = _ssd(proj, dt_raw, conv_left, ssd_init, lp, nb, seq, width, cols)
    if cache_kv is not None:
        cache_kv = tuple(t.reshape(nb * WINDOW, kvw) for t in cache_kv)
    o, kn = _attn(proj, cols, cache_kv, lp, nb, seq, width)
    x1 = _mm_out(y, o, lp["w_out"], x2, g1, seq, boff)
    h2 = _norm_mod(x1, lp["g_ffn"], sc2, sh2, seq, boff)
    act = _mm_gate_up(h2, lp["w_gu"])
    out = _mm_down(act, lp["w_dn"], x1, g2, seq, boff)
    keep = min(WINDOW, seq) if cache_kv is None else seq
    heads = width // HEAD_DIM
    k_state = kn.reshape(nb, seq, KV_HEADS, HEAD_DIM)[:, seq - keep:]
    v_state = proj[:, cols["v"]:cols["v"] + kvw].astype(F32).reshape(nb, seq, KV_HEADS, HEAD_DIM)[:, seq - keep:]
    ssd_state = ssd_state.reshape(nb, heads, HEAD_DIM, SSD_STATE)
    return out.reshape(nb, seq, d), ssd_state, conv_state, k_state, v_state


def kernel(x_prompt, x_sample, state_ssd, state_conv, cache_k, cache_v, c_prompt, c_sample, w_ada, b_ada, g_mix, w_in, conv_w, conv_b, dt_bias, a_log, d_skip, ssd_norm_w, q_norm_w, k_norm_w, sinks, w_out, g_ffn, w_gate_up, w_down):
    depth = w_in.shape[0]
    bp, _, d = x_prompt.shape
    bs = x_sample.shape[0]
    width = d // 2
    assert cache_k.shape[2] == WINDOW and x_sample.shape[1] <= CHUNK
    yp, ys = x_prompt, x_sample
    outs = [[] for _ in range(8)]
    c_all = jnp.concatenate([c_prompt, c_sample], axis=0)
    zero_conv = jnp.zeros((bp, SSD_CONV - 1, width + 2 * SSD_GROUPS * SSD_STATE), F32)
    for l in range(depth):
        lp = _prep_layer(l, w_ada, b_ada, g_mix, w_in, conv_w, conv_b, dt_bias, a_log, d_skip, ssd_norm_w,
                         q_norm_w, k_norm_w, sinks, w_out, g_ffn, w_gate_up, w_down)
        mod = _ada(c_all, lp["w_ada"], lp["b_ada"]).reshape(bp + bs, N_MOD, 1, d)
        mods = [mod[:, i] for i in range(N_MOD)]
        yp, s1, s2, s3, s4 = _layer(yp, mods, 0, lp, None, zero_conv, None)
        init = state_ssd[l].reshape(bs, width, SSD_STATE)
        ys, t1, t2, t3, t4 = _layer(ys, mods, bp, lp, init, state_conv[l], (cache_k[l], cache_v[l]))
        for lst, v in zip(outs, (s1, s2, s3, s4, t1, t2, t3, t4)):
            lst.append(v)
    stacked = [jnp.stack(v, axis=0) for v in outs]
    return (yp, ys, *stacked)
```

```python
import functools
import math

import numpy as np
import jax
import jax.numpy as jnp
from jax import lax
from jax.experimental import pallas as pl
from jax.experimental.pallas import tpu as pltpu

CHUNK = 64
HEAD_DIM = 64
SSD_GROUPS = 4
SSD_STATE = 128
SSD_CONV = 4
KV_HEADS = 4
WINDOW = 128
N_MOD = 6
EPS = 1e-6
LOG2E = math.log2(math.e)

LANES = 128
SUBLANES = 8
BF16_ROWS = 16
MXU_WIDTH = 256
VMEM_LIMIT_BYTES = 56 * 1024 * 1024

PACK_STRIDE = 32

F32 = jnp.float32
BF16 = jnp.bfloat16


def _cparams(*sem):
    return pltpu.CompilerParams(dimension_semantics=sem, vmem_limit_bytes=VMEM_LIMIT_BYTES)


def _pick(n, candidates):
    for c in candidates:
        if c <= n and n % c == 0:
            return c
    return n


def _silu(x):
    h = 0.5 * x
    return h * jnp.tanh(h) + h


def _dot(a, b):
    return jnp.dot(a, b, preferred_element_type=F32)


def _dot_nt(a, b):
    return lax.dot_general(a, b, (((1,), (1,)), ((), ())), preferred_element_type=F32)


def _dot_tn(a, b):
    return lax.dot_general(a, b, (((0,), (0,)), ((), ())), preferred_element_type=F32)


def _split3(v):
    hi = v.astype(BF16)
    r1 = v - hi.astype(F32)
    mid = r1.astype(BF16)
    lo = (r1 - mid.astype(F32)).astype(BF16)
    return hi, mid, lo


def _dot01_left(m01, v):
    hi, mid, lo = _split3(v)
    return _dot(m01, hi) + (_dot(m01, mid) + _dot(m01, lo))


def _pack3(v):
    hi, mid, lo = _split3(v)
    packed = hi.astype(F32) + pltpu.roll(mid.astype(F32), PACK_STRIDE, 1) + pltpu.roll(lo.astype(F32), 2 * PACK_STRIDE, 1)
    return packed.astype(BF16)


def _ada_kernel(c_ref, w_ref, b_ref, o_ref):
    a = _silu(c_ref[...]).astype(BF16)
    o_ref[...] = _dot(a, w_ref[...].astype(BF16)) + b_ref[...]


def _ada(c, w, b):
    r, d = c.shape
    n = w.shape[1]
    tn = _pick(n, (512, 256, 128))
    return pl.pallas_call(
        _ada_kernel,
        grid=(n // tn,),
        in_specs=[pl.BlockSpec((r, d), lambda j: (0, 0)),
                  pl.BlockSpec((d, tn), lambda j: (0, j)),
                  pl.BlockSpec((1, tn), lambda j: (0, j))],
        out_specs=pl.BlockSpec((r, tn), lambda j: (0, j)),
        out_shape=jax.ShapeDtypeStruct((r, n), F32),
        compiler_params=_cparams("arbitrary"),
        name="ada",
    )(c, w, b.reshape(1, n))


def _norm_kernel(x_ref, g_ref, sc_ref, sh_ref, o_ref):
    x = x_ref[...]
    r = lax.rsqrt(jnp.mean(x * x, axis=-1, keepdims=True) + EPS)
    y = x * r * g_ref[...]
    o_ref[...] = (y * (1.0 + sc_ref[...]) + sh_ref[...]).astype(o_ref.dtype)


def _norm_mod(x, g, sc, sh, seq, boff):
    m, d = x.shape
    tm = _pick(seq, (256, 128, 64, 32, 16, 8))
    per = seq // tm
    mod_spec = pl.BlockSpec((None, 1, d), lambda i: (boff + i // per, 0, 0))
    return pl.pallas_call(
        _norm_kernel,
        grid=(m // tm,),
        in_specs=[pl.BlockSpec((tm, d), lambda i: (i, 0)),
                  pl.BlockSpec((1, d), lambda i: (0, 0)),
                  mod_spec, mod_spec],
        out_specs=pl.BlockSpec((tm, d), lambda i: (i, 0)),
        out_shape=jax.ShapeDtypeStruct((m, d), BF16),
        compiler_params=_cparams("arbitrary"),
        name="norm_mod",
    )(x, g.reshape(1, d), sc, sh)


def _mm_in_kernel(a_ref, wa_ref, wb_ref, wdt_ref, o_ref, dt_ref, *, na):
    a = a_ref[...]
    j = pl.program_id(1)

    @pl.when(j < na)
    def _():
        o_ref[...] = _dot(a, wa_ref[...]).astype(o_ref.dtype)

    @pl.when(j >= na)
    def _():
        o_ref[...] = _dot(a, wb_ref[...]).astype(o_ref.dtype)

    @pl.when(j == 0)
    def _():
        dt_ref[...] = _dot(a, wdt_ref[...])


def _mm_in(a, w_a, a_cols, w_b, w_dt, dest, tn):
    m, k = a.shape
    na, nbt = a_cols // tn, w_b.shape[1] // tn
    assert a_cols % tn == 0 and w_b.shape[1] % tn == 0 and sorted(dest) == list(range(na + nbt))
    tm = _pick(m, (1024, 512, 256, 128))

    def out_tile(j):
        r = dest[0]
        for t in range(1, na + nbt):
            r = jnp.where(j >= t, dest[t], r)
        return r

    return pl.pallas_call(
        functools.partial(_mm_in_kernel, na=na),
        grid=(m // tm, na + nbt),
        in_specs=[pl.BlockSpec((tm, k), lambda i, j: (i, 0)),
                  pl.BlockSpec((k, tn), lambda i, j: (0, jnp.minimum(j, na - 1))),
                  pl.BlockSpec((k, tn), lambda i, j: (0, jnp.maximum(j - na, 0))),
                  pl.BlockSpec((k, LANES), lambda i, j: (0, 0))],
        out_specs=[pl.BlockSpec((tm, tn), lambda i, j: (i, out_tile(j))),
                   pl.BlockSpec((tm, LANES), lambda i, j: (i, 0))],
        out_shape=[jax.ShapeDtypeStruct((m, (na + nbt) * tn), BF16),
                   jax.ShapeDtypeStruct((m, LANES), F32)],
        compiler_params=_cparams("arbitrary", "arbitrary"),
        name="mm_in",
    )(a, w_a, w_b, w_dt)


def _gate_operand(gate, tm, tn, seq, boff, nb):
    if seq % tm == 0:
        per = seq // tm
        return gate, pl.BlockSpec((None, 1, tn), lambda i, j: (boff + i // per, 0, j))
    rows = jnp.repeat(gate[boff:boff + nb, 0], seq, axis=0)
    return rows, pl.BlockSpec((tm, tn), lambda i, j: (i, j))


def _mm_out_kernel(y_ref, o_ref, wa_ref, wb_ref, x_ref, g_ref, out_ref):
    acc = _dot(y_ref[...], wa_ref[...]) + _dot(o_ref[...], wb_ref[...])
    out_ref[...] = x_ref[...] + g_ref[...] * acc


def _mm_out(y, o, w, x, gate, seq, boff):
    m, wd = y.shape
    d = w.shape[1]
    tm = _pick(m, (1024, 512, 256, 128))
    tn = _pick(d, (512, 256, 128))
    gate, gate_spec = _gate_operand(gate, tm, tn, seq, boff, m // seq)
    return pl.pallas_call(
        _mm_out_kernel,
        grid=(m // tm, d // tn),
        in_specs=[pl.BlockSpec((tm, wd), lambda i, j: (i, 0)),
                  pl.BlockSpec((tm, wd), lambda i, j: (i, 0)),
                  pl.BlockSpec((wd, tn), lambda i, j: (0, j)),
                  pl.BlockSpec((wd, tn), lambda i, j: (1, j)),
                  pl.BlockSpec((tm, tn), lambda i, j: (i, j)),
                  gate_spec],
        out_specs=pl.BlockSpec((tm, tn), lambda i, j: (i, j)),
        out_shape=jax.ShapeDtypeStruct((m, d), F32),
        compiler_params=_cparams("arbitrary", "arbitrary"),
        name="mm_out",
    )(y, o, w, w, x, gate)


def _mm_gu_kernel(h_ref, wg_ref, wu_ref, o_ref):
    h = h_ref[...]
    g = _dot(h, wg_ref[...])
    u = _dot(h, wu_ref[...])
    o_ref[...] = (_silu(g) * u).astype(o_ref.dtype)


def _mm_gate_up(h, w):
    m, d = h.shape
    ff = w.shape[1] // 2
    tm = _pick(m, (2048, 1024, 512, 256, 128))
    tn = _pick(ff, (512, 256, 128))
    nj = ff // tn
    return pl.pallas_call(
        _mm_gu_kernel,
        grid=(m // tm, nj),
        in_specs=[pl.BlockSpec((tm, d), lambda i, j: (i, 0)),
                  pl.BlockSpec((d, tn), lambda i, j: (0, j)),
                  pl.BlockSpec((d, tn), lambda i, j: (0, nj + j))],
        out_specs=pl.BlockSpec((tm, tn), lambda i, j: (i, j)),
        out_shape=jax.ShapeDtypeStruct((m, ff), BF16),
        compiler_params=_cparams("arbitrary", "arbitrary"),
        name="mm_gate_up",
    )(h, w, w)


def _mm_down_kernel(a_ref, w_ref, x_ref, g_ref, o_ref):
    o_ref[...] = x_ref[...] + g_ref[...] * _dot(a_ref[...], w_ref[...])


def _mm_down(a, w, x, gate, seq, boff):
    m, f = a.shape
    d = w.shape[1]
    tm = _pick(m, (512, 256, 128))
    tn = _pick(d, (512, 256, 128))
    gate, gate_spec = _gate_operand(gate, tm, tn, seq, boff, m // seq)
    return pl.pallas_call(
        _mm_down_kernel,
        grid=(m // tm, d // tn),
        in_specs=[pl.BlockSpec((tm, f), lambda i, j: (i, 0)),
                  pl.BlockSpec((f, tn), lambda i, j: (0, j)),
                  pl.BlockSpec((tm, tn), lambda i, j: (i, j)),
                  gate_spec],
        out_specs=pl.BlockSpec((tm, tn), lambda i, j: (i, j)),
        out_shape=jax.ShapeDtypeStruct((m, d), F32),
        compiler_params=_cparams("arbitrary", "arbitrary"),
        name="mm_down",
    )(a, w, x, gate)


LEFT_ROWS = 3 * BF16_ROWS
SHIFT_ROWS = SSD_CONV * CHUNK + SUBLANES


def _shift_matrix(valid):
    keep = SSD_CONV - 1
    t = np.zeros((SHIFT_ROWS, LEFT_ROWS + valid), np.float32)
    for k in range(SSD_CONV):
        for r in range(valid):
            src = r + k - keep
            if src >= 0:
                t[k * CHUNK + r, LEFT_ROWS + src] = 1.0
            else:
                for piece in range(3):
                    t[k * CHUNK + r, piece * BF16_ROWS + BF16_ROWS + src] = 1.0
    for r in range(keep):
        t[SSD_CONV * CHUNK + r, LEFT_ROWS + valid - keep + r] = 1.0
    return jnp.asarray(t, BF16)


def _ssd_stages(refs, valid, hpg, has_init):
    if has_init:
        (z_ref, x_ref, b_ref, c_ref, dt_ref, convl_ref, init_ref, tmat_ref, cw_ref, cb_ref, dtb_ref, alog_ref,
         dskip_ref, nw_ref, sel_ref, y_ref, state_ref, convs_ref, left, tmp, st) = refs
    else:
        (z_ref, x_ref, b_ref, c_ref, dt_ref, convl_ref, tmat_ref, cw_ref, cb_ref, dtb_ref, alog_ref,
         dskip_ref, nw_ref, sel_ref, y_ref, state_ref, convs_ref, left, tmp, st) = refs
        init_ref = None
    c = pl.program_id(1)
    width = x_ref.shape[1]
    heads = width // HEAD_DIM
    gn = SSD_GROUPS * SSD_STATE
    gw = hpg * HEAD_DIM
    L = CHUNK
    keep = SSD_CONV - 1

    @pl.when(c == 0)
    def _():
        tmp[...] = jnp.zeros_like(tmp)
        tmp[BF16_ROWS - keep:BF16_ROWS, :] = convl_ref[...]
        hi, mid, lo = _split3(tmp[...])
        left[0:BF16_ROWS, :] = lo
        left[BF16_ROWS:2 * BF16_ROWS, :] = mid
        left[2 * BF16_ROWS:LEFT_ROWS, :] = hi
        if has_init:
            st[...] = init_ref[...].T
        else:
            st[...] = jnp.zeros_like(st)

    tmat = tmat_ref[...]
    bounds = ((0, width), (width, width + gn), (width + gn, width + 2 * gn))
    shifted = [_dot(tmat, jnp.concatenate([left[:, lo:hi], ref[...]], axis=0))
               for ref, (lo, hi) in zip((x_ref, b_ref, c_ref), bounds)]
    yield

    def conv(sh, lo, hi):
        acc = cb_ref[:, lo:hi] + sh[0:L] * cw_ref[0:1, lo:hi]
        for k in range(1, SSD_CONV):
            acc = acc + sh[k * L:(k + 1) * L] * cw_ref[k:k + 1, lo:hi]
        convs_ref[:, lo:hi] = sh[SSD_CONV * L:SSD_CONV * L + keep]
        return _silu(acc)

    xs, bc, cc = [conv(sh, lo, hi) for sh, (lo, hi) in zip(shifted, bounds)]

    left[0:2 * BF16_ROWS, :] = jnp.zeros((2 * BF16_ROWS, left.shape[1]), BF16)
    left[2 * BF16_ROWS:LEFT_ROWS, 0:width] = x_ref[valid - BF16_ROWS:valid, :]
    left[2 * BF16_ROWS:LEFT_ROWS, width:width + gn] = b_ref[valid - BF16_ROWS:valid, :]
    left[2 * BF16_ROWS:LEFT_ROWS, width + gn:width + 2 * gn] = c_ref[valid - BF16_ROWS:valid, :]

    row = lax.broadcasted_iota(jnp.int32, (L, LANES), 0)
    lane = lax.broadcasted_iota(jnp.int32, (L, LANES), 1)
    if valid < L:
        dt_raw = jnp.concatenate([dt_ref[...], jnp.zeros((L - valid, LANES), F32)], axis=0)
    else:
        dt_raw = dt_ref[...]
    u = dt_raw + dtb_ref[...]
    dt = jnp.maximum(u, 0.0) + jnp.log1p(jnp.exp(-jnp.abs(u)))
    dt = jnp.where((row < valid) & (lane < heads), dt, 0.0)
    a = dt * (-jnp.exp(alog_ref[...]))

    tri = jnp.where(lax.broadcasted_iota(jnp.int32, (L, L), 0) >= lax.broadcasted_iota(jnp.int32, (L, L), 1),
                    1.0, 0.0).astype(BF16)
    a_cs = _dot01_left(tri, a)
    both = _dot(jnp.concatenate([_pack3(a_cs), _pack3(dt)], axis=0), sel_ref[...])
    yield
    acs_b = both[0:L]
    dt_b = both[L:2 * L]

    s_idx = lane & (HEAD_DIM - 1)
    causal = jnp.where(row >= s_idx, 0.0, -jnp.inf)
    diag = row == s_idx
    dec = []
    for j in range(width // LANES):
        blk = acs_b[:, j * LANES:(j + 1) * LANES]
        acs_row = jnp.sum(jnp.where(diag, blk, 0.0), axis=0, keepdims=True)
        dec.append(jnp.exp(blk - acs_row + causal))
    decay = jnp.concatenate(dec, axis=1)
    last_b = acs_b[L - 1:L, :]
    e_in = jnp.exp(acs_b)
    e_out = jnp.exp(last_b - acs_b)
    e_last = jnp.exp(last_b)

    xbar = xs * dt_b
    xd16 = (xbar * e_out).astype(BF16)

    pr = lax.broadcasted_iota(jnp.int32, (2 * HEAD_DIM, 2 * HEAD_DIM), 0) >= HEAD_DIM
    pc = lax.broadcasted_iota(jnp.int32, (2 * HEAD_DIM, 2 * HEAD_DIM), 1) >= HEAD_DIM
    pair_mask = pr == pc

    zg = _silu(z_ref[...].astype(F32))
    groups = range(SSD_GROUPS)
    span = lambda g: slice(g * gw, (g + 1) * gw)
    bgs = [bc[:, g * SSD_STATE:(g + 1) * SSD_STATE].astype(BF16) for g in groups]
    cgs = [cc[:, g * SSD_STATE:(g + 1) * SSD_STATE].astype(BF16) for g in groups]
    cb_ts = [_dot_nt(cgs[g], jnp.concatenate([bgs[g]] * hpg, axis=0)) for g in groups]
    st_gs = [st[:, span(g)] for g in groups]
    y_offs = [_dot(cgs[g], st_gs[g].astype(BF16)) for g in groups]
    st_new = [_dot_tn(bgs[g], xd16[:, span(g)]) for g in groups]
    yield
    for g in groups:
        st[:, span(g)] = st_gs[g] * e_last[:, span(g)] + st_new[g]
    y_diags = []
    for g in groups:
        gmat = (cb_ts[g] * decay[:, span(g)]).astype(BF16)
        for p in range(hpg // 2):
            plo = g * gw + p * 2 * HEAD_DIM
            xp = xbar[:, plo:plo + 2 * HEAD_DIM]
            bd = jnp.where(pair_mask, jnp.concatenate([xp, xp], axis=0), 0.0).astype(BF16)
            y_diags.append(_dot(gmat[:, p * 2 * HEAD_DIM:(p + 1) * 2 * HEAD_DIM], bd))
    yield
    for g in groups:
        y_diag = jnp.concatenate(y_diags[g * (hpg // 2):(g + 1) * (hpg // 2)], axis=1)
        y = y_diag + y_offs[g] * e_in[:, span(g)] + dskip_ref[:, span(g)] * xs[:, span(g)]
        gt = y[0:valid, :] * zg[:, span(g)]
        rs = lax.rsqrt(jnp.mean(gt * gt, axis=-1, keepdims=True) + EPS)
        y_ref[:, span(g)] = (gt * rs * nw_ref[:, span(g)]).astype(y_ref.dtype)

    @pl.when(c == pl.num_programs(1) - 1)
    def _():
        state_ref[...] = st[...].T


def _ssd_parts(proj, dt_raw, conv_left, init_state, lp, nb, seq, width, cols):
    m = proj.shape[0]
    valid = min(CHUNK, seq)
    nc = seq // valid
    heads = width // HEAD_DIM
    hpg = heads // SSD_GROUPS
    gn = SSD_GROUPS * SSD_STATE
    cch = width + 2 * gn
    has_init = init_state is not None
    assert valid % BF16_ROWS == 0 and hpg % 2 == 0 and heads <= PACK_STRIDE and 3 * PACK_STRIDE <= LANES

    def rows(b, c):
        return b * nc + c

    def col_spec(w, off):
        assert off % w == 0
        return pl.BlockSpec((valid, w), lambda b, c: (rows(b, c), off // w))

    full = lambda shape: pl.BlockSpec(shape, lambda b, c: tuple(0 for _ in shape))
    in_specs = [col_spec(width, cols["z"]), col_spec(width, cols["x"]),
                col_spec(gn, cols["B"]), col_spec(gn, cols["C"]),
                pl.BlockSpec((valid, LANES), lambda b, c: (rows(b, c), 0)),
                pl.BlockSpec((None, SSD_CONV - 1, cch), lambda b, c: (b, 0, 0))]
    args = [proj, proj, proj, proj, dt_raw, conv_left]
    if has_init:
        in_specs.append(pl.BlockSpec((None, width, SSD_STATE), lambda b, c: (b, 0, 0)))
        args.append(init_state)
    tmat = _shift_matrix(valid)
    in_specs += [full(tmat.shape), full((SSD_CONV, cch)), full((1, cch)), full((1, LANES)), full((1, LANES)),
                 full((1, width)), full((1, width)), full((LANES, width))]
    args += [tmat, lp["conv_w"], lp["conv_b"], lp["dt_bias"], lp["a_log"], lp["d_skip"], lp["ssd_norm_w"], lp["sel"]]
    stages = functools.partial(_ssd_stages, valid=valid, hpg=hpg, has_init=has_init)
    out_specs = [pl.BlockSpec((valid, width), lambda b, c: (rows(b, c), 0)),
                 pl.BlockSpec((None, width, SSD_STATE), lambda b, c: (b, 0, 0)),
                 pl.BlockSpec((None, SSD_CONV - 1, cch), lambda b, c: (b, 0, 0))]
    out_shapes = [jax.ShapeDtypeStruct((m, width), BF16),
                  jax.ShapeDtypeStruct((nb, width, SSD_STATE), F32),
                  jax.ShapeDtypeStruct((nb, SSD_CONV - 1, cch), F32)]
    scratch = [pltpu.VMEM((LEFT_ROWS, cch), BF16),
               pltpu.VMEM((BF16_ROWS, cch), F32),
               pltpu.VMEM((SSD_STATE, width), F32)]
    return stages, in_specs, args, out_specs, out_shapes, scratch


def _head_rms(x, w):
    return x * lax.rsqrt(jnp.mean(x * x, axis=-1, keepdims=True) + EPS) * w


def _attn_stages(refs, lq, qpk, hb, prompt):
    if prompt:
        q_ref, ko_ref, vo_ref = refs[:3]
        rest = refs[3:]
    else:
        q_ref, ko_ref, vo_ref, kp_ref, vp_ref = refs[:5]
        rest = refs[5:]
    kw_ref, ks_ref, sink_ref, dup_ref, eye_ref, mq_ref, mo_ref, ep_ref, o_ref, kn_ref, krings, vrings, bias = rest
    b = pl.program_id(0)
    c = pl.program_id(1)
    nk = WINDOW + lq
    heads = KV_HEADS * qpk
    wc = WINDOW // CHUNK
    cur = lax.rem(c, 2)
    kring = krings.at[cur]
    vring = vrings.at[cur]

    @pl.when((b == 0) & (c == 0))
    def _():
        j = lax.broadcasted_iota(jnp.int32, (nk, heads * lq), 1)
        s = lax.broadcasted_iota(jnp.int32, (nk, heads * lq), 0)
        h = lax.shift_right_logical(j, lq.bit_length() - 1)
        l_idx = j & (lq - 1)
        slope = jnp.exp2(-8.0 * (h + 1).astype(F32) / heads)
        bias[...] = -(slope * LOG2E) * jnp.abs(WINDOW + l_idx - s).astype(F32)

    if prompt:
        @pl.when(c == 0)
        def _():
            krings[...] = jnp.zeros_like(krings)
            vrings[...] = jnp.zeros_like(vrings)
    else:
        kring[0:WINDOW, :] = (kp_ref[...] * ks_ref[...]).astype(BF16)
        vring[0:WINDOW, :] = vp_ref[...].astype(BF16)

    yield
    kw = kw_ref[...]
    for g in range(KV_HEADS):
        lo, hi = g * HEAD_DIM, (g + 1) * HEAD_DIM
        kn_ref[:, lo:hi] = _head_rms(ko_ref[:, lo:hi].astype(F32), kw)
    kring[WINDOW:nk, :] = (kn_ref[...] * ks_ref[...]).astype(BF16)
    vring[WINDOW:nk, :] = vo_ref[...].astype(BF16)

    hw = hb * HEAD_DIM
    pw = hb * lq
    kdup = _dot(kring[...], dup_ref[...]).astype(BF16)
    vt = _dot_nt(eye_ref[...], vring[...]).astype(BF16)
    mask_q = mq_ref[...]
    mask_o = mo_ref[...]
    eye_blk = ep_ref[...]
    ones = jnp.ones((SUBLANES, hw), BF16)
    yield
    if prompt:
        offs = [jnp.where(c >= wc - i, 0.0, jnp.inf) for i in range(wc)]
    pairs = range(heads // hb)
    group = lambda pi: pi // (qpk // hb)
    qbs = []
    for pi in pairs:
        qp = q_ref[:, pi * hw:(pi + 1) * hw]
        qbs.append(jnp.concatenate([qp] * hb, axis=0) * mask_q)
    ssqs = [_dot_nt(ones, qb * qb)[0:1] for qb in qbs]
    raw = [_dot_nt(kdup[:, group(pi) * hw:(group(pi) + 1) * hw], qbs[pi]) for pi in pairs]
    yield
    pns = []
    for pi in pairs:
        rq = lax.rsqrt(ssqs[pi] * (1.0 / HEAD_DIM) + EPS)
        s2 = raw[pi] * rq + bias[:, pi * pw:(pi + 1) * pw]
        sink2 = sink_ref[:, pi * pw:(pi + 1) * pw]
        m_own = jnp.maximum(jnp.max(s2[WINDOW:nk], axis=0, keepdims=True), sink2)
        if prompt:
            mx = m_own
            for i in range(wc):
                band = s2[i * CHUNK:(i + 1) * CHUNK]
                mx = jnp.maximum(mx, jnp.max(band, axis=0, keepdims=True) - offs[i])
            p = jnp.concatenate([jnp.exp2(s2[i * CHUNK:(i + 1) * CHUNK] - (mx + offs[i])) for i in range(wc)]
                                + [jnp.exp2(s2[WINDOW:nk] - mx)], axis=0)
        else:
            mx = jnp.maximum(m_own, jnp.max(s2[0:WINDOW], axis=0, keepdims=True))
            p = jnp.exp2(s2 - mx)
        den = jnp.sum(p, axis=0, keepdims=True) + jnp.exp2(sink2 - mx)
        pns.append((p * (1.0 / den)).astype(BF16))
    ots = [_dot(vt[group(pi) * HEAD_DIM:(group(pi) + 1) * HEAD_DIM, :], pns[pi]).astype(BF16)
           for pi in pairs]
    yield
    obs = [jnp.concatenate([ot] * hb, axis=0) * mask_o for ot in ots]
    outs = [_dot_nt(eye_blk, ob) for ob in obs]
    for pi in pairs:
        o_ref[:, pi * hw:(pi + 1) * hw] = outs[pi].astype(o_ref.dtype)

    if prompt:
        krings.at[1 - cur][0:WINDOW, :] = kring[lq:nk, :]
        vrings.at[1 - cur][0:WINDOW, :] = vring[lq:nk, :]


def _attn_parts(proj, cols, cache_kv, lp, nb, seq, width):
    m = nb * seq
    lq = min(CHUNK, seq)
    nc = seq // lq
    heads = width // HEAD_DIM
    qpk = heads // KV_HEADS
    kvw = KV_HEADS * HEAD_DIM
    prompt = cache_kv is None
    nk = WINDOW + lq
    assert qpk % 2 == 0 and lq & (lq - 1) == 0 and lq % BF16_ROWS == 0 and WINDOW % CHUNK == 0
    assert cols["q"] % width == 0 and cols["k"] % kvw == 0 and cols["v"] % kvw == 0
    assert prompt and lq == CHUNK or not prompt and nc == 1
    row_spec = lambda w, off: pl.BlockSpec((lq, w), lambda b, c: (b * nc + c, off // w))
    full = lambda shape: pl.BlockSpec(shape, lambda b, c: tuple(0 for _ in shape))
    in_specs = [row_spec(width, cols["q"]), row_spec(kvw, cols["k"]), row_spec(kvw, cols["v"])]
    args = [proj, proj, proj]
    if not prompt:
        in_specs += [pl.BlockSpec((WINDOW, kvw), lambda b, c: (b, 0))] * 2
        args += list(cache_kv)
    sink2 = (jnp.repeat(lp["sinks"], lq) * LOG2E).reshape(1, heads * lq)
    hb = MXU_WIDTH // HEAD_DIM if qpk % (MXU_WIDTH // HEAD_DIM) == 0 else 2
    hw = hb * HEAD_DIM
    kv_idx = np.arange(kvw)
    dup_idx = np.arange(KV_HEADS * hw)
    dup = (kv_idx[:, None] // HEAD_DIM == dup_idx[None, :] // hw) & (kv_idx[:, None] % HEAD_DIM == dup_idx[None, :] % HEAD_DIM)
    ql = np.arange(hb * lq)
    dl = np.arange(hw)
    mask_q = ql[:, None] // lq == dl[None, :] // HEAD_DIM
    eye_blk = np.arange(lq)[:, None] == ql[None, :] % lq
    consts = [jnp.asarray(a, BF16) for a in (dup, np.eye(kvw), mask_q, mask_q.T, eye_blk)]
    in_specs += [full((1, HEAD_DIM)), full((1, kvw)), full((1, heads * lq))] + [full(a.shape) for a in consts]
    args += [lp["k_norm_w"], lp["k_scale"], sink2] + consts
    stages = functools.partial(_attn_stages, lq=lq, qpk=qpk, hb=hb, prompt=prompt)
    out_specs = [pl.BlockSpec((lq, width), lambda b, c: (b * nc + c, 0)),
                 pl.BlockSpec((lq, kvw), lambda b, c: (b * nc + c, 0))]
    out_shapes = [jax.ShapeDtypeStruct((m, width), BF16),
                  jax.ShapeDtypeStruct((m, kvw), F32)]
    scratch = [pltpu.VMEM((2, nk, kvw), BF16),
               pltpu.VMEM((2, nk, kvw), BF16),
               pltpu.VMEM((nk, heads * lq), F32)]
    return stages, in_specs, args, out_specs, out_shapes, scratch


def _mixer_kernel(*refs, ssd_stages, attn_stages, n_ssd, n_attn):
    (si, so, ss), (ai, ao, as_) = n_ssd, n_attn
    ins, outs, scr = refs[:si + ai], refs[si + ai:si + ai + so + ao], refs[si + ai + so + ao:]
    ssd = ssd_stages(ins[:si] + outs[:so] + scr[:ss])
    attn = attn_stages(ins[si:] + outs[so:] + scr[ss:])
    for gen in (attn, ssd, attn, attn, ssd, ssd, attn, ssd, attn, ssd):
        next(gen, None)
    for gen in (ssd, attn):
        assert next(gen, "done") == "done"


def _mixer(proj, dt_raw, conv_left, init_state, cache_kv, lp, nb, seq, width, cols):
    s_st, s_in, s_args, s_out, s_shape, s_scr = _ssd_parts(proj, dt_raw, conv_left, init_state, lp, nb, seq, width, cols)
    a_st, a_in, a_args, a_out, a_shape, a_scr = _attn_parts(proj, cols, cache_kv, lp, nb, seq, width)
    kern = functools.partial(_mixer_kernel, ssd_stages=s_st, attn_stages=a_st,
                             n_ssd=(len(s_in), len(s_out), len(s_scr)), n_attn=(len(a_in), len(a_out), len(a_scr)))
    return pl.pallas_call(
        kern,
        grid=(nb, seq // min(CHUNK, seq)),
        in_specs=s_in + a_in,
        out_specs=s_out + a_out,
        out_shape=s_shape + a_shape,
        scratch_shapes=s_scr + a_scr,
        compiler_params=_cparams("arbitrary", "arbitrary"),
        name="mixer",
    )(*s_args, *a_args)


def _prep_layer(l, w_ada, b_ada, g_mix, w_in, conv_w, conv_b, dt_bias, a_log, d_skip, ssd_norm_w,
                q_norm_w, k_norm_w, sinks, w_out, g_ffn, w_gate_up, w_down):
    d = w_in.shape[1]
    width = d // 2
    gn = SSD_GROUPS * SSD_STATE
    kvw = KV_HEADS * HEAD_DIM
    heads = width // HEAD_DIM
    wi = w_in[l]
    o1 = width
    o2 = o1 + width + 2 * gn
    o3 = o2 + heads
    w_all = wi.astype(BF16)
    w_b = w_all[:, o3:]
    w_dt = jnp.pad(w_all[:, o2:o3], ((0, 0), (0, LANES - heads)))
    cols = {"z": 0, "x": width, "q": 2 * width, "B": 3 * width, "C": 3 * width + gn,
            "k": 3 * width + 2 * gn, "v": 3 * width + 2 * gn + kvw}
    tn = 2 * kvw
    assert width % tn == 0 and gn % tn == 0
    src_order = ["z", "x", "B", "C", "q", "k"]
    src_width = {"z": width, "x": width, "B": gn, "C": gn, "q": width, "k": 2 * kvw}
    dest = [cols[name] // tn + t for name in src_order for t in range(src_width[name] // tn)]
    lane_piece = jnp.arange(LANES)
    head_of_lane = jnp.arange(width) // HEAD_DIM
    sel = ((lane_piece[:, None] % PACK_STRIDE == head_of_lane[None, :])
           & (lane_piece[:, None] < 3 * PACK_STRIDE)).astype(BF16)
    pad_h = (0, LANES - heads)
    k_scale = jnp.tile(q_norm_w[l], KV_HEADS) * (HEAD_DIM ** -0.5 * LOG2E)
    return dict(
        w_ada=w_ada[l], b_ada=b_ada[l], g_mix=g_mix[l], g_ffn=g_ffn[l],
        w_a=w_all, a_cols=o2, w_b=w_b, w_dt=w_dt, cols=cols, dest=dest, tn_in=tn, w_out=w_out[l].astype(BF16),
        w_gu=w_gate_up[l].astype(BF16), w_dn=w_down[l].astype(BF16),
        conv_w=conv_w[l], conv_b=conv_b[l].reshape(1, -1),
        dt_bias=jnp.pad(dt_bias[l], pad_h).reshape(1, LANES), a_log=jnp.pad(a_log[l], pad_h).reshape(1, LANES),
        d_skip=jnp.repeat(d_skip[l], HEAD_DIM).reshape(1, width), ssd_norm_w=ssd_norm_w[l].reshape(1, width),
        sel=sel, k_norm_w=k_norm_w[l].reshape(1, HEAD_DIM), k_scale=k_scale.reshape(1, kvw),
        sinks=sinks[l], width=width)


def _layer(x, mods, boff, lp, ssd_init, conv_left, cache_kv):
    nb, seq, d = x.shape
    m = nb * seq
    width = lp["width"]
    cols = lp["cols"]
    kvw = KV_HEADS * HEAD_DIM
    sh1, sc1, g1, sh2, sc2, g2 = mods
    x2 = x.reshape(m, d)
    h = _norm_mod(x2, lp["g_mix"], sc1, sh1, seq, boff)
    proj, dt_raw = _mm_in(h, lp["w_a"], lp["a_cols"], lp["w_b"], lp["w_dt"], lp["dest"], lp["tn_in"])
    if cache_kv is not None:
        cache_kv = tuple(t.reshape(nb * WINDOW, kvw) for t in cache_kv)
    y, ssd_state, conv_state, o, kn = _mixer(proj, dt_raw, conv_left, ssd_init, cache_kv, lp, nb, seq, width, cols)
    x1 = _mm_out(y, o, lp["w_out"], x2, g1, seq, boff)
    h2 = _norm_mod(x1, lp["g_ffn"], sc2, sh2, seq, boff)
    act = _mm_gate_up(h2, lp["w_gu"])
    out = _mm_down(act, lp["w_dn"], x1, g2, seq, boff)
    keep = min(WINDOW, seq) if cache_kv is None else seq
    heads = width // HEAD_DIM
    k_state = kn.reshape(nb, seq, KV_HEADS, HEAD_DIM)[:, seq - keep:]
    v_state = proj[:, cols["v"]:cols["v"] + kvw].astype(F32).reshape(nb, seq, KV_HEADS, HEAD_DIM)[:, seq - keep:]
    ssd_state = ssd_state.reshape(nb, heads, HEAD_DIM, SSD_STATE)
    return out.reshape(nb, seq, d), ssd_state, conv_state, k_state, v_state


def kernel(x_prompt, x_sample, state_ssd, state_conv, cache_k, cache_v, c_prompt, c_sample, w_ada, b_ada, g_mix, w_in, conv_w, conv_b, dt_bias, a_log, d_skip, ssd_norm_w, q_norm_w, k_norm_w, sinks, w_out, g_ffn, w_gate_up, w_down):
    depth = w_in.shape[0]
    bp, _, d = x_prompt.shape
    bs = x_sample.shape[0]
    width = d // 2
    assert cache_k.shape[2] == WINDOW and x_sample.shape[1] <= CHUNK
    yp, ys = x_prompt, x_sample
    outs = [[] for _ in range(8)]
    c_all = jnp.concatenate([c_prompt, c_sample], axis=0)
    zero_conv = jnp.zeros((bp, SSD_CONV - 1, width + 2 * SSD_GROUPS * SSD_STATE), F32)
    for l in range(depth):
        lp = _prep_layer(l, w_ada, b_ada, g_mix, w_in, conv_w, conv_b, dt_bias, a_log, d_skip, ssd_norm_w,
                         q_norm_w, k_norm_w, sinks, w_out, g_ffn, w_gate_up, w_down)
        mod = _ada(c_all, lp["w_ada"], lp["b_ada"]).reshape(bp + bs, N_MOD, 1, d)
        mods = [mod[:, i] for i in range(N_MOD)]
        yp, s1, s2, s3, s4 = _layer(yp, mods, 0, lp, None, zero_conv, None)
        init = state_ssd[l].reshape(bs, width, SSD_STATE)
        ys, t1, t2, t3, t4 = _layer(ys, mods, bp, lp, init, state_conv[l], (cache_k[l], cache_v[l]))
        for lst, v in zip(outs, (s1, s2, s3, s4, t1, t2, t3, t4)):
            lst.append(v)
    stacked = [jnp.stack(v, axis=0) for v in outs]
    return (yp, ys, *stacked)
```

```python
import functools
import math

import numpy as np
import jax
import jax.numpy as jnp
from jax import lax
from jax.experimental import pallas as pl
from jax.experimental.pallas import tpu as pltpu

CHUNK = 64
HEAD_DIM = 64
SSD_GROUPS = 4
SSD_STATE = 128
SSD_CONV = 4
KV_HEADS = 4
WINDOW = 128
N_MOD = 6
EPS = 1e-6
LOG2E = math.log2(math.e)

LANES = 128
SUBLANES = 8
BF16_ROWS = 16
MXU_WIDTH = 256
VMEM_LIMIT_BYTES = 56 * 1024 * 1024

PACK_STRIDE = 32

F32 = jnp.float32
BF16 = jnp.bfloat16


def _cparams(*sem):
    return pltpu.CompilerParams(dimension_semantics=sem, vmem_limit_bytes=VMEM_LIMIT_BYTES)


def _pick(n, candidates):
    for c in candidates:
        if c <= n and n % c == 0:
            return c
    return n


def _silu(x):
    h = 0.5 * x
    return h * jnp.tanh(h) + h


def _dot(a, b):
    return jnp.dot(a, b, preferred_element_type=F32)


def _dot_nt(a, b):
    return lax.dot_general(a, b, (((1,), (1,)), ((), ())), preferred_element_type=F32)


def _dot_tn(a, b):
    return lax.dot_general(a, b, (((0,), (0,)), ((), ())), preferred_element_type=F32)


def _split3(v):
    hi = v.astype(BF16)
    r1 = v - hi.astype(F32)
    mid = r1.astype(BF16)
    lo = (r1 - mid.astype(F32)).astype(BF16)
    return hi, mid, lo


def _dot01_left(m01, v):
    hi, mid, lo = _split3(v)
    return _dot(m01, hi) + (_dot(m01, mid) + _dot(m01, lo))


def _pack3(v):
    hi, mid, lo = _split3(v)
    packed = hi.astype(F32) + pltpu.roll(mid.astype(F32), PACK_STRIDE, 1) + pltpu.roll(lo.astype(F32), 2 * PACK_STRIDE, 1)
    return packed.astype(BF16)


def _ada_kernel(c_ref, w_ref, b_ref, o_ref):
    a = _silu(c_ref[...]).astype(BF16)
    o_ref[...] = _dot(a, w_ref[...].astype(BF16)) + b_ref[...]


def _ada(c, w, b):
    r, d = c.shape
    n = w.shape[1]
    tn = _pick(n, (512, 256, 128))
    return pl.pallas_call(
        _ada_kernel,
        grid=(n // tn,),
        in_specs=[pl.BlockSpec((r, d), lambda j: (0, 0)),
                  pl.BlockSpec((d, tn), lambda j: (0, j)),
                  pl.BlockSpec((1, tn), lambda j: (0, j))],
        out_specs=pl.BlockSpec((r, tn), lambda j: (0, j)),
        out_shape=jax.ShapeDtypeStruct((r, n), F32),
        compiler_params=_cparams("arbitrary"),
        name="ada",
    )(c, w, b.reshape(1, n))


def _norm_kernel(x_ref, g_ref, sc_ref, sh_ref, o_ref):
    x = x_ref[...]
    r = lax.rsqrt(jnp.mean(x * x, axis=-1, keepdims=True) + EPS)
    y = x * r * g_ref[...]
    o_ref[...] = (y * (1.0 + sc_ref[...]) + sh_ref[...]).astype(o_ref.dtype)


def _norm_mod(x, g, sc, sh, seq, boff):
    m, d = x.shape
    tm = _pick(seq, (256, 128, 64, 32, 16, 8))
    per = seq // tm
    mod_spec = pl.BlockSpec((None, 1, d), lambda i: (boff + i // per, 0, 0))
    return pl.pallas_call(
        _norm_kernel,
        grid=(m // tm,),
        in_specs=[pl.BlockSpec((tm, d), lambda i: (i, 0)),
                  pl.BlockSpec((1, d), lambda i: (0, 0)),
                  mod_spec, mod_spec],
        out_specs=pl.BlockSpec((tm, d), lambda i: (i, 0)),
        out_shape=jax.ShapeDtypeStruct((m, d), BF16),
        compiler_params=_cparams("arbitrary"),
        name="norm_mod",
    )(x, g.reshape(1, d), sc, sh)


def _mm_in_kernel(a_ref, wa_ref, wb_ref, wdt_ref, o_ref, dt_ref, *, na):
    a = a_ref[...]
    j = pl.program_id(1)
    w = jnp.where(j < na, wa_ref[...], wb_ref[...])
    o_ref[...] = _dot(a, w).astype(o_ref.dtype)

    @pl.when(j == 0)
    def _():
        dt_ref[...] = _dot(a, wdt_ref[...])


def _mm_in(a, w_a, a_cols, w_b, w_dt, dest, tn):
    m, k = a.shape
    na, nbt = a_cols // tn, w_b.shape[1] // tn
    assert a_cols % tn == 0 and w_b.shape[1] % tn == 0 and sorted(dest) == list(range(na + nbt))
    tm = _pick(m, (1024, 512, 256, 128))

    def out_tile(j):
        r = dest[0]
        for t in range(1, na + nbt):
            r = jnp.where(j >= t, dest[t], r)
        return r

    return pl.pallas_call(
        functools.partial(_mm_in_kernel, na=na),
        grid=(m // tm, na + nbt),
        in_specs=[pl.BlockSpec((tm, k), lambda i, j: (i, 0)),
                  pl.BlockSpec((k, tn), lambda i, j: (0, jnp.minimum(j, na - 1))),
                  pl.BlockSpec((k, tn), lambda i, j: (0, jnp.maximum(j - na, 0))),
                  pl.BlockSpec((k, LANES), lambda i, j: (0, 0))],
        out_specs=[pl.BlockSpec((tm, tn), lambda i, j: (i, out_tile(j))),
                   pl.BlockSpec((tm, LANES), lambda i, j: (i, 0))],
        out_shape=[jax.ShapeDtypeStruct((m, (na + nbt) * tn), BF16),
                   jax.ShapeDtypeStruct((m, LANES), F32)],
        compiler_params=_cparams("arbitrary", "arbitrary"),
        name="mm_in",
    )(a, w_a, w_b, w_dt)


def _gate_operand(gate, tm, tn, seq, boff, nb, ij=lambda i, j: (i, j)):
    if seq % tm == 0:
        per = seq // tm
        return gate, pl.BlockSpec((None, 1, tn), lambda *g: (boff + ij(*g)[0] // per, 0, ij(*g)[1]))
    rows = jnp.repeat(gate[boff:boff + nb, 0], seq, axis=0)
    return rows, pl.BlockSpec((tm, tn), lambda *g: ij(*g))


def _mm_out_kernel(y_ref, o_ref, wa_ref, wb_ref, x_ref, g_ref, out_ref):
    acc = _dot(y_ref[...], wa_ref[...]) + _dot(o_ref[...], wb_ref[...])
    out_ref[...] = x_ref[...] + g_ref[...] * acc


def _mm_out(y, o, w, x, gate, seq, boff):
    m, wd = y.shape
    d = w.shape[1]
    tm = _pick(m, (1024, 512, 256, 128))
    tn = _pick(d, (512, 256, 128))
    gate, gate_spec = _gate_operand(gate, tm, tn, seq, boff, m // seq)
    return pl.pallas_call(
        _mm_out_kernel,
        grid=(m // tm, d // tn),
        in_specs=[pl.BlockSpec((tm, wd), lambda i, j: (i, 0)),
                  pl.BlockSpec((tm, wd), lambda i, j: (i, 0)),
                  pl.BlockSpec((wd, tn), lambda i, j: (0, j)),
                  pl.BlockSpec((wd, tn), lambda i, j: (1, j)),
                  pl.BlockSpec((tm, tn), lambda i, j: (i, j)),
                  gate_spec],
        out_specs=pl.BlockSpec((tm, tn), lambda i, j: (i, j)),
        out_shape=jax.ShapeDtypeStruct((m, d), F32),
        compiler_params=_cparams("arbitrary", "arbitrary"),
        name="mm_out",
    )(y, o, w, w, x, gate)


def _mm_gu_kernel(h_ref, wg_ref, wu_ref, o_ref):
    h = h_ref[...]
    g = _dot(h, wg_ref[...])
    u = _dot(h, wu_ref[...])
    o_ref[...] = (_silu(g) * u).astype(o_ref.dtype)


def _mm_gate_up(h, w):
    m, d = h.shape
    ff = w.shape[1] // 2
    tm = _pick(m, (2048, 1024, 512, 256, 128))
    tn = _pick(ff, (512, 256, 128))
    nj = ff // tn
    return pl.pallas_call(
        _mm_gu_kernel,
        grid=(m // tm, nj),
        in_specs=[pl.BlockSpec((tm, d), lambda i, j: (i, 0)),
                  pl.BlockSpec((d, tn), lambda i, j: (0, j)),
                  pl.BlockSpec((d, tn), lambda i, j: (0, nj + j))],
        out_specs=pl.BlockSpec((tm, tn), lambda i, j: (i, j)),
        out_shape=jax.ShapeDtypeStruct((m, ff), BF16),
        compiler_params=_cparams("arbitrary", "arbitrary"),
        name="mm_gate_up",
    )(h, w, w)


def _mm_down_kernel(a_ref, w_ref, x_ref, g_ref, o_ref):
    o_ref[...] = x_ref[...] + g_ref[...] * _dot(a_ref[...], w_ref[...])


def _mm_down(a, w, x, gate, seq, boff):
    m, f = a.shape
    d = w.shape[1]
    tm = _pick(m, (512, 256, 128))
    tn = _pick(d, (512, 256, 128))
    gate, gate_spec = _gate_operand(gate, tm, tn, seq, boff, m // seq, ij=lambda j, i: (i, j))
    return pl.pallas_call(
        _mm_down_kernel,
        grid=(d // tn, m // tm),
        in_specs=[pl.BlockSpec((tm, f), lambda j, i: (i, 0)),
                  pl.BlockSpec((f, tn), lambda j, i: (0, j)),
                  pl.BlockSpec((tm, tn), lambda j, i: (i, j)),
                  gate_spec],
        out_specs=pl.BlockSpec((tm, tn), lambda j, i: (i, j)),
        out_shape=jax.ShapeDtypeStruct((m, d), F32),
        compiler_params=_cparams("arbitrary", "arbitrary"),
        name="mm_down",
    )(a, w, x, gate)


LEFT_ROWS = 3 * BF16_ROWS
SHIFT_ROWS = SSD_CONV * CHUNK + SUBLANES


def _shift_matrix(valid):
    keep = SSD_CONV - 1
    t = np.zeros((SHIFT_ROWS, LEFT_ROWS + valid), np.float32)
    for k in range(SSD_CONV):
        for r in range(valid):
            src = r + k - keep
            if src >= 0:
                t[k * CHUNK + r, LEFT_ROWS + src] = 1.0
            else:
                for piece in range(3):
                    t[k * CHUNK + r, piece * BF16_ROWS + BF16_ROWS + src] = 1.0
    for r in range(keep):
        t[SSD_CONV * CHUNK + r, LEFT_ROWS + valid - keep + r] = 1.0
    return jnp.asarray(t, BF16)


def _ssd_stages(refs, valid, hpg, has_init):
    if has_init:
        (z_ref, x_ref, b_ref, c_ref, dt_ref, convl_ref, init_ref, tmat_ref, cw_ref, cb_ref, dtb_ref, alog_ref,
         dskip_ref, nw_ref, sel_ref, y_ref, state_ref, convs_ref, left, tmp, st) = refs
    else:
        (z_ref, x_ref, b_ref, c_ref, dt_ref, convl_ref, tmat_ref, cw_ref, cb_ref, dtb_ref, alog_ref,
         dskip_ref, nw_ref, sel_ref, y_ref, state_ref, convs_ref, left, tmp, st) = refs
        init_ref = None
    c = pl.program_id(1)
    width = x_ref.shape[1]
    heads = width // HEAD_DIM
    gn = SSD_GROUPS * SSD_STATE
    gw = hpg * HEAD_DIM
    L = CHUNK
    keep = SSD_CONV - 1

    @pl.when(c == 0)
    def _():
        tmp[...] = jnp.zeros_like(tmp)
        tmp[BF16_ROWS - keep:BF16_ROWS, :] = convl_ref[...]
        hi, mid, lo = _split3(tmp[...])
        left[0:BF16_ROWS, :] = lo
        left[BF16_ROWS:2 * BF16_ROWS, :] = mid
        left[2 * BF16_ROWS:LEFT_ROWS, :] = hi
        if has_init:
            st[...] = init_ref[...].T
        else:
            st[...] = jnp.zeros_like(st)

    tmat = tmat_ref[...]
    bounds = ((0, width), (width, width + gn), (width + gn, width + 2 * gn))
    shifted = [_dot(tmat, jnp.concatenate([left[:, lo:hi], ref[...]], axis=0))
               for ref, (lo, hi) in zip((x_ref, b_ref, c_ref), bounds)]
    yield

    def conv(sh, lo, hi):
        acc = cb_ref[:, lo:hi] + sh[0:L] * cw_ref[0:1, lo:hi]
        for k in range(1, SSD_CONV):
            acc = acc + sh[k * L:(k + 1) * L] * cw_ref[k:k + 1, lo:hi]
        convs_ref[:, lo:hi] = sh[SSD_CONV * L:SSD_CONV * L + keep]
        return _silu(acc)

    xs, bc, cc = [conv(sh, lo, hi) for sh, (lo, hi) in zip(shifted, bounds)]

    left[0:2 * BF16_ROWS, :] = jnp.zeros((2 * BF16_ROWS, left.shape[1]), BF16)
    left[2 * BF16_ROWS:LEFT_ROWS, 0:width] = x_ref[valid - BF16_ROWS:valid, :]
    left[2 * BF16_ROWS:LEFT_ROWS, width:width + gn] = b_ref[valid - BF16_ROWS:valid, :]
    left[2 * BF16_ROWS:LEFT_ROWS, width + gn:width + 2 * gn] = c_ref[valid - BF16_ROWS:valid, :]

    row = lax.broadcasted_iota(jnp.int32, (L, LANES), 0)
    lane = lax.broadcasted_iota(jnp.int32, (L, LANES), 1)
    if valid < L:
        dt_raw = jnp.concatenate([dt_ref[...], jnp.zeros((L - valid, LANES), F32)], axis=0)
    else:
        dt_raw = dt_ref[...]
    u = dt_raw + dtb_ref[...]
    dt = jnp.maximum(u, 0.0) + jnp.log1p(jnp.exp(-jnp.abs(u)))
    dt = jnp.where((row < valid) & (lane < heads), dt, 0.0)
    a = dt * (-jnp.exp(alog_ref[...]))

    tri = jnp.where(lax.broadcasted_iota(jnp.int32, (L, L), 0) >= lax.broadcasted_iota(jnp.int32, (L, L), 1),
                    1.0, 0.0).astype(BF16)
    a_cs = _dot01_left(tri, a)
    both = _dot(jnp.concatenate([_pack3(a_cs), _pack3(dt)], axis=0), sel_ref[...])
    yield
    acs_b = both[0:L]
    dt_b = both[L:2 * L]

    s_idx = lane & (HEAD_DIM - 1)
    causal = jnp.where(row >= s_idx, 0.0, -jnp.inf)
    diag = row == s_idx
    dec = []
    for j in range(width // LANES):
        blk = acs_b[:, j * LANES:(j + 1) * LANES]
        acs_row = jnp.sum(jnp.where(diag, blk, 0.0), axis=0, keepdims=True)
        dec.append(jnp.exp(blk - acs_row + causal))
    decay = jnp.concatenate(dec, axis=1)
    last_b = acs_b[L - 1:L, :]
    e_in = jnp.exp(acs_b)
    e_out = jnp.exp(last_b - acs_b)
    e_last = jnp.exp(last_b)

    xbar = xs * dt_b
    xd16 = (xbar * e_out).astype(BF16)

    pr = lax.broadcasted_iota(jnp.int32, (2 * HEAD_DIM, 2 * HEAD_DIM), 0) >= HEAD_DIM
    pc = lax.broadcasted_iota(jnp.int32, (2 * HEAD_DIM, 2 * HEAD_DIM), 1) >= HEAD_DIM
    pair_mask = pr == pc

    zg = _silu(z_ref[...].astype(F32))
    groups = range(SSD_GROUPS)
    span = lambda g: slice(g * gw, (g + 1) * gw)
    bgs = [bc[:, g * SSD_STATE:(g + 1) * SSD_STATE].astype(BF16) for g in groups]
    cgs = [cc[:, g * SSD_STATE:(g + 1) * SSD_STATE].astype(BF16) for g in groups]
    cb_ts = [_dot_nt(cgs[g], jnp.concatenate([bgs[g]] * hpg, axis=0)) for g in groups]
    st_gs = [st[:, span(g)] for g in groups]
    y_offs = [_dot(cgs[g], st_gs[g].astype(BF16)) for g in groups]
    st_new = [_dot_tn(bgs[g], xd16[:, span(g)]) for g in groups]
    yield
    for g in groups:
        st[:, span(g)] = st_gs[g] * e_last[:, span(g)] + st_new[g]
    y_diags = []
    for g in groups:
        gmat = (cb_ts[g] * decay[:, span(g)]).astype(BF16)
        for p in range(hpg // 2):
            plo = g * gw + p * 2 * HEAD_DIM
            xp = xbar[:, plo:plo + 2 * HEAD_DIM]
            bd = jnp.where(pair_mask, jnp.concatenate([xp, xp], axis=0), 0.0).astype(BF16)
            y_diags.append(_dot(gmat[:, p * 2 * HEAD_DIM:(p + 1) * 2 * HEAD_DIM], bd))
    yield
    for g in groups:
        y_diag = jnp.concatenate(y_diags[g * (hpg // 2):(g + 1) * (hpg // 2)], axis=1)
        y = y_diag + y_offs[g] * e_in[:, span(g)] + dskip_ref[:, span(g)] * xs[:, span(g)]
        gt = y[0:valid, :] * zg[:, span(g)]
        rs = lax.rsqrt(jnp.mean(gt * gt, axis=-1, keepdims=True) + EPS)
        y_ref[:, span(g)] = (gt * rs * nw_ref[:, span(g)]).astype(y_ref.dtype)

    @pl.when(c == pl.num_programs(1) - 1)
    def _():
        state_ref[...] = st[...].T


def _ssd_parts(proj, dt_raw, conv_left, init_state, lp, nb, seq, width, cols):
    m = proj.shape[0]
    valid = min(CHUNK, seq)
    nc = seq // valid
    heads = width // HEAD_DIM
    hpg = heads // SSD_GROUPS
    gn = SSD_GROUPS * SSD_STATE
    cch = width + 2 * gn
    has_init = init_state is not None
    assert valid % BF16_ROWS == 0 and hpg % 2 == 0 and heads <= PACK_STRIDE and 3 * PACK_STRIDE <= LANES

    def rows(b, c):
        return b * nc + c

    def col_spec(w, off):
        assert off % w == 0
        return pl.BlockSpec((valid, w), lambda b, c: (rows(b, c), off // w))

    full = lambda shape: pl.BlockSpec(shape, lambda b, c: tuple(0 for _ in shape))
    in_specs = [col_spec(width, cols["z"]), col_spec(width, cols["x"]),
                col_spec(gn, cols["B"]), col_spec(gn, cols["C"]),
                pl.BlockSpec((valid, LANES), lambda b, c: (rows(b, c), 0)),
                pl.BlockSpec((None, SSD_CONV - 1, cch), lambda b, c: (b, 0, 0))]
    args = [proj, proj, proj, proj, dt_raw, conv_left]
    if has_init:
        in_specs.append(pl.BlockSpec((None, width, SSD_STATE), lambda b, c: (b, 0, 0)))
        args.append(init_state)
    tmat = _shift_matrix(valid)
    in_specs += [full(tmat.shape), full((SSD_CONV, cch)), full((1, cch)), full((1, LANES)), full((1, LANES)),
                 full((1, width)), full((1, width)), full((LANES, width))]
    args += [tmat, lp["conv_w"], lp["conv_b"], lp["dt_bias"], lp["a_log"], lp["d_skip"], lp["ssd_norm_w"], lp["sel"]]
    stages = functools.partial(_ssd_stages, valid=valid, hpg=hpg, has_init=has_init)
    out_specs = [pl.BlockSpec((valid, width), lambda b, c: (rows(b, c), 0)),
                 pl.BlockSpec((None, width, SSD_STATE), lambda b, c: (b, 0, 0)),
                 pl.BlockSpec((None, SSD_CONV - 1, cch), lambda b, c: (b, 0, 0))]
    out_shapes = [jax.ShapeDtypeStruct((m, width), BF16),
                  jax.ShapeDtypeStruct((nb, width, SSD_STATE), F32),
                  jax.ShapeDtypeStruct((nb, SSD_CONV - 1, cch), F32)]
    scratch = [pltpu.VMEM((LEFT_ROWS, cch), BF16),
               pltpu.VMEM((BF16_ROWS, cch), F32),
               pltpu.VMEM((SSD_STATE, width), F32)]
    return stages, in_specs, args, out_specs, out_shapes, scratch


def _head_rms(x, w):
    return x * lax.rsqrt(jnp.mean(x * x, axis=-1, keepdims=True) + EPS) * w


def _attn_stages(refs, lq, qpk, hb, prompt):
    if prompt:
        q_ref, ko_ref, vo_ref = refs[:3]
        rest = refs[3:]
    else:
        q_ref, ko_ref, vo_ref, kp_ref, vp_ref = refs[:5]
        rest = refs[5:]
    kw_ref, ks_ref, sink_ref, dup_ref, eye_ref, mq_ref, mo_ref, ep_ref, o_ref, kn_ref, krings, vrings, bias = rest
    b = pl.program_id(0)
    c = pl.program_id(1)
    nk = WINDOW + lq
    heads = KV_HEADS * qpk
    wc = WINDOW // CHUNK
    cur = lax.rem(c, 2)
    kring = krings.at[cur]
    vring = vrings.at[cur]

    @pl.when((b == 0) & (c == 0))
    def _():
        j = lax.broadcasted_iota(jnp.int32, (nk, heads * lq), 1)
        s = lax.broadcasted_iota(jnp.int32, (nk, heads * lq), 0)
        h = lax.shift_right_logical(j, lq.bit_length() - 1)
        l_idx = j & (lq - 1)
        slope = jnp.exp2(-8.0 * (h + 1).astype(F32) / heads)
        bias[...] = -(slope * LOG2E) * jnp.abs(WINDOW + l_idx - s).astype(F32)

    if prompt:
        @pl.when(c == 0)
        def _():
            krings[...] = jnp.zeros_like(krings)
            vrings[...] = jnp.zeros_like(vrings)
    else:
        kring[0:WINDOW, :] = (kp_ref[...] * ks_ref[...]).astype(BF16)
        vring[0:WINDOW, :] = vp_ref[...].astype(BF16)

    yield
    kw = kw_ref[...]
    for g in range(KV_HEADS):
        lo, hi = g * HEAD_DIM, (g + 1) * HEAD_DIM
        kn_ref[:, lo:hi] = _head_rms(ko_ref[:, lo:hi].astype(F32), kw)
    kring[WINDOW:nk, :] = (kn_ref[...] * ks_ref[...]).astype(BF16)
    vring[WINDOW:nk, :] = vo_ref[...].astype(BF16)

    hw = hb * HEAD_DIM
    pw = hb * lq
    kdup = _dot(kring[...], dup_ref[...]).astype(BF16)
    vt = _dot_nt(eye_ref[...], vring[...]).astype(BF16)
    mask_q = mq_ref[...]
    mask_o = mo_ref[...]
    eye_blk = ep_ref[...]
    ones = jnp.ones((SUBLANES, hw), BF16)
    yield
    if prompt:
        offs = [jnp.where(c >= wc - i, 0.0, jnp.inf) for i in range(wc)]
    pairs = range(heads // hb)
    group = lambda pi: pi // (qpk // hb)
    qbs = []
    for pi in pairs:
        qp = q_ref[:, pi * hw:(pi + 1) * hw]
        qbs.append(jnp.concatenate([qp] * hb, axis=0) * mask_q)
    ssqs = [_dot_nt(ones, qb * qb)[0:1] for qb in qbs]
    raw = [_dot_nt(kdup[:, group(pi) * hw:(group(pi) + 1) * hw], qbs[pi]) for pi in pairs]
    yield
    pns = []
    for pi in pairs:
        rq = lax.rsqrt(ssqs[pi] * (1.0 / HEAD_DIM) + EPS)
        s2 = raw[pi] * rq + bias[:, pi * pw:(pi + 1) * pw]
        sink2 = sink_ref[:, pi * pw:(pi + 1) * pw]
        m_own = jnp.maximum(jnp.max(s2[WINDOW:nk], axis=0, keepdims=True), sink2)
        if prompt:
            mx = m_own
            for i in range(wc):
                band = s2[i * CHUNK:(i + 1) * CHUNK]
                mx = jnp.maximum(mx, jnp.max(band, axis=0, keepdims=True) - offs[i])
            p = jnp.concatenate([jnp.exp2(s2[i * CHUNK:(i + 1) * CHUNK] - (mx + offs[i])) for i in range(wc)]
                                + [jnp.exp2(s2[WINDOW:nk] - mx)], axis=0)
        else:
            mx = jnp.maximum(m_own, jnp.max(s2[0:WINDOW], axis=0, keepdims=True))
            p = jnp.exp2(s2 - mx)
        den = jnp.sum(p, axis=0, keepdims=True) + jnp.exp2(sink2 - mx)
        pns.append((p * (1.0 / den)).astype(BF16))
    ots = [_dot(vt[group(pi) * HEAD_DIM:(group(pi) + 1) * HEAD_DIM, :], pns[pi]).astype(BF16)
           for pi in pairs]
    yield
    obs = [jnp.concatenate([ot] * hb, axis=0) * mask_o for ot in ots]
    outs = [_dot_nt(eye_blk, ob) for ob in obs]
    for pi in pairs:
        o_ref[:, pi * hw:(pi + 1) * hw] = outs[pi].astype(o_ref.dtype)

    if prompt:
        krings.at[1 - cur][0:WINDOW, :] = kring[lq:nk, :]
        vrings.at[1 - cur][0:WINDOW, :] = vring[lq:nk, :]


def _attn_parts(proj, cols, cache_kv, lp, nb, seq, width):
    m = nb * seq
    lq = min(CHUNK, seq)
    nc = seq // lq
    heads = width // HEAD_DIM
    qpk = heads // KV_HEADS
    kvw = KV_HEADS * HEAD_DIM
    prompt = cache_kv is None
    nk = WINDOW + lq
    assert qpk % 2 == 0 and lq & (lq - 1) == 0 and lq % BF16_ROWS == 0 and WINDOW % CHUNK == 0
    assert cols["q"] % width == 0 and cols["k"] % kvw == 0 and cols["v"] % kvw == 0
    assert prompt and lq == CHUNK or not prompt and nc == 1
    row_spec = lambda w, off: pl.BlockSpec((lq, w), lambda b, c: (b * nc + c, off // w))
    full = lambda shape: pl.BlockSpec(shape, lambda b, c: tuple(0 for _ in shape))
    in_specs = [row_spec(width, cols["q"]), row_spec(kvw, cols["k"]), row_spec(kvw, cols["v"])]
    args = [proj, proj, proj]
    if not prompt:
        in_specs += [pl.BlockSpec((WINDOW, kvw), lambda b, c: (b, 0))] * 2
        args += list(cache_kv)
    sink2 = (jnp.repeat(lp["sinks"], lq) * LOG2E).reshape(1, heads * lq)
    hb = MXU_WIDTH // HEAD_DIM if qpk % (MXU_WIDTH // HEAD_DIM) == 0 else 2
    hw = hb * HEAD_DIM
    kv_idx = np.arange(kvw)
    dup_idx = np.arange(KV_HEADS * hw)
    dup = (kv_idx[:, None] // HEAD_DIM == dup_idx[None, :] // hw) & (kv_idx[:, None] % HEAD_DIM == dup_idx[None, :] % HEAD_DIM)
    ql = np.arange(hb * lq)
    dl = np.arange(hw)
    mask_q = ql[:, None] // lq == dl[None, :] // HEAD_DIM
    eye_blk = np.arange(lq)[:, None] == ql[None, :] % lq
    consts = [jnp.asarray(a, BF16) for a in (dup, np.eye(kvw), mask_q, mask_q.T, eye_blk)]
    in_specs += [full((1, HEAD_DIM)), full((1, kvw)), full((1, heads * lq))] + [full(a.shape) for a in consts]
    args += [lp["k_norm_w"], lp["k_scale"], sink2] + consts
    stages = functools.partial(_attn_stages, lq=lq, qpk=qpk, hb=hb, prompt=prompt)
    out_specs = [pl.BlockSpec((lq, width), lambda b, c: (b * nc + c, 0)),
                 pl.BlockSpec((lq, kvw), lambda b, c: (b * nc + c, 0))]
    out_shapes = [jax.ShapeDtypeStruct((m, width), BF16),
                  jax.ShapeDtypeStruct((m, kvw), F32)]
    scratch = [pltpu.VMEM((2, nk, kvw), BF16),
               pltpu.VMEM((2, nk, kvw), BF16),
               pltpu.VMEM((nk, heads * lq), F32)]
    return stages, in_specs, args, out_specs, out_shapes, scratch


def _mixer_kernel(*refs, ssd_stages, attn_stages, n_ssd, n_attn):
    (si, so, ss), (ai, ao, as_) = n_ssd, n_attn
    ins, outs, scr = refs[:si + ai], refs[si + ai:si + ai + so + ao], refs[si + ai + so + ao:]
    ssd = ssd_stages(ins[:si] + outs[:so] + scr[:ss])
    attn = attn_stages(ins[si:] + outs[so:] + scr[ss:])
    for gen in (attn, ssd, attn, attn, ssd, ssd, attn, ssd, attn, ssd):
        next(gen, None)
    for gen in (ssd, attn):
        assert next(gen, "done") == "done"


def _mixer(proj, dt_raw, conv_left, init_state, cache_kv, lp, nb, seq, width, cols):
    s_st, s_in, s_args, s_out, s_shape, s_scr = _ssd_parts(proj, dt_raw, conv_left, init_state, lp, nb, seq, width, cols)
    a_st, a_in, a_args, a_out, a_shape, a_scr = _attn_parts(proj, cols, cache_kv, lp, nb, seq, width)
    kern = functools.partial(_mixer_kernel, ssd_stages=s_st, attn_stages=a_st,
                             n_ssd=(len(s_in), len(s_out), len(s_scr)), n_attn=(len(a_in), len(a_out), len(a_scr)))
    return pl.pallas_call(
        kern,
        grid=(nb, seq // min(CHUNK, seq)),
        in_specs=s_in + a_in,
        out_specs=s_out + a_out,
        out_shape=s_shape + a_shape,
        scratch_shapes=s_scr + a_scr,
        compiler_params=_cparams("arbitrary", "arbitrary"),
        name="mixer",
    )(*s_args, *a_args)


def _prep_layer(l, w_ada, b_ada, g_mix, w_in, conv_w, conv_b, dt_bias, a_log, d_skip, ssd_norm_w,
                q_norm_w, k_norm_w, sinks, w_out, g_ffn, w_gate_up, w_down):
    d = w_in.shape[1]
    width = d // 2
    gn = SSD_GROUPS * SSD_STATE
    kvw = KV_HEADS * HEAD_DIM
    heads = width // HEAD_DIM
    wi = w_in[l]
    o1 = width
    o2 = o1 + width + 2 * gn
    o3 = o2 + heads
    w_all = wi.astype(BF16)
    w_b = w_all[:, o3:]
    w_dt = jnp.pad(w_all[:, o2:o3], ((0, 0), (0, LANES - heads)))
    cols = {"z": 0, "x": width, "q": 2 * width, "B": 3 * width, "C": 3 * width + gn,
            "k": 3 * width + 2 * gn, "v": 3 * width + 2 * gn + kvw}
    tn = 2 * kvw
    assert width % tn == 0 and gn % tn == 0
    src_order = ["z", "x", "B", "C", "q", "k"]
    src_width = {"z": width, "x": width, "B": gn, "C": gn, "q": width, "k": 2 * kvw}
    dest = [cols[name] // tn + t for name in src_order for t in range(src_width[name] // tn)]
    lane_piece = jnp.arange(LANES)
    head_of_lane = jnp.arange(width) // HEAD_DIM
    sel = ((lane_piece[:, None] % PACK_STRIDE == head_of_lane[None, :])
           & (lane_piece[:, None] < 3 * PACK_STRIDE)).astype(BF16)
    pad_h = (0, LANES - heads)
    k_scale = jnp.tile(q_norm_w[l], KV_HEADS) * (HEAD_DIM ** -0.5 * LOG2E)
    return dict(
        w_ada=w_ada[l], b_ada=b_ada[l], g_mix=g_mix[l], g_ffn=g_ffn[l],
        w_a=w_all, a_cols=o2, w_b=w_b, w_dt=w_dt, cols=cols, dest=dest, tn_in=tn, w_out=w_out[l].astype(BF16),
        w_gu=w_gate_up[l].astype(BF16), w_dn=w_down[l].astype(BF16),
        conv_w=conv_w[l], conv_b=conv_b[l].reshape(1, -1),
        dt_bias=jnp.pad(dt_bias[l], pad_h).reshape(1, LANES), a_log=jnp.pad(a_log[l], pad_h).reshape(1, LANES),
        d_skip=jnp.repeat(d_skip[l], HEAD_DIM).reshape(1, width), ssd_norm_w=ssd_norm_w[l].reshape(1, width),
        sel=sel, k_norm_w=k_norm_w[l].reshape(1, HEAD_DIM), k_scale=k_scale.reshape(1, kvw),
        sinks=sinks[l], width=width)


def _layer(x, mods, boff, lp, ssd_init, conv_left, cache_kv):
    nb, seq, d = x.shape
    m = nb * seq
    width = lp["width"]
    cols = lp["cols"]
    kvw = KV_HEADS * HEAD_DIM
    sh1, sc1, g1, sh2, sc2, g2 = mods
    x2 = x.reshape(m, d)
    h = _norm_mod(x2, lp["g_mix"], sc1, sh1, seq, boff)
    proj, dt_raw = _mm_in(h, lp["w_a"], lp["a_cols"], lp["w_b"], lp["w_dt"], lp["dest"], lp["tn_in"])
    if cache_kv is not None:
        cache_kv = tuple(t.reshape(nb * WINDOW, kvw) for t in cache_kv)
    y, ssd_state, conv_state, o, kn = _mixer(proj, dt_raw, conv_left, ssd_init, cache_kv, lp, nb, seq, width, cols)
    x1 = _mm_out(y, o, lp["w_out"], x2, g1, seq, boff)
    h2 = _norm_mod(x1, lp["g_ffn"], sc2, sh2, seq, boff)
    act = _mm_gate_up(h2, lp["w_gu"])
    out = _mm_down(act, lp["w_dn"], x1, g2, seq, boff)
    keep = min(WINDOW, seq) if cache_kv is None else seq
    heads = width // HEAD_DIM
    k_state = kn.reshape(nb, seq, KV_HEADS, HEAD_DIM)[:, seq - keep:]
    v_state = proj[:, cols["v"]:cols["v"] + kvw].astype(F32).reshape(nb, seq, KV_HEADS, HEAD_DIM)[:, seq - keep:]
    ssd_state = ssd_state.reshape(nb, heads, HEAD_DIM, SSD_STATE)
    return out.reshape(nb, seq, d), ssd_state, conv_state, k_state, v_state


def kernel(x_prompt, x_sample, state_ssd, state_conv, cache_k, cache_v, c_prompt, c_sample, w_ada, b_ada, g_mix, w_in, conv_w, conv_b, dt_bias, a_log, d_skip, ssd_norm_w, q_norm_w, k_norm_w, sinks, w_out, g_ffn, w_gate_up, w_down):
    depth = w_in.shape[0]
    bp, _, d = x_prompt.shape
    bs = x_sample.shape[0]
    width = d // 2
    assert cache_k.shape[2] == WINDOW and x_sample.shape[1] <= CHUNK
    yp, ys = x_prompt, x_sample
    outs = [[] for _ in range(8)]
    c_all = jnp.concatenate([c_prompt, c_sample], axis=0)
    zero_conv = jnp.zeros((bp, SSD_CONV - 1, width + 2 * SSD_GROUPS * SSD_STATE), F32)
    for l in range(depth):
        lp = _prep_layer(l, w_ada, b_ada, g_mix, w_in, conv_w, conv_b, dt_bias, a_log, d_skip, ssd_norm_w,
                         q_norm_w, k_norm_w, sinks, w_out, g_ffn, w_gate_up, w_down)
        mod = _ada(c_all, lp["w_ada"], lp["b_ada"]).reshape(bp + bs, N_MOD, 1, d)
        mods = [mod[:, i] for i in range(N_MOD)]
        yp, s1, s2, s3, s4 = _layer(yp, mods, 0, lp, None, zero_conv, None)
        init = state_ssd[l].reshape(bs, width, SSD_STATE)
        ys, t1, t2, t3, t4 = _layer(ys, mods, bp, lp, init, state_conv[l], (cache_k[l], cache_v[l]))
        for lst, v in zip(outs, (s1, s2, s3, s4, t1, t2, t3, t4)):
            lst.append(v)
    stacked = [jnp.stack(v, axis=0) for v in outs]
    return (yp, ys, *stacked)
```

```python
import functools
import math

import numpy as np
import jax
import jax.numpy as jnp
from jax import lax
from jax.experimental import pallas as pl
from jax.experimental.pallas import tpu as pltpu

CHUNK = 64
HEAD_DIM = 64
SSD_GROUPS = 4
SSD_STATE = 128
SSD_CONV = 4
KV_HEADS = 4
WINDOW = 128
N_MOD = 6
EPS = 1e-6
LOG2E = math.log2(math.e)

LANES = 128
SUBLANES = 8
BF16_ROWS = 16
MXU_WIDTH = 256
VMEM_LIMIT_BYTES = 56 * 1024 * 1024

PACK_STRIDE = 32

F32 = jnp.float32
BF16 = jnp.bfloat16


def _cparams(*sem):
    return pltpu.CompilerParams(dimension_semantics=sem, vmem_limit_bytes=VMEM_LIMIT_BYTES)


def _pick(n, candidates):
    for c in candidates:
        if c <= n and n % c == 0:
            return c
    return n


def _cast_slab_specs(ws, steps, step_of):
    specs = []
    for w in ws:
        k, n = w.shape
        if k % steps or (k // steps) % BF16_ROWS:
            return None
        specs.append(pl.BlockSpec((k // steps, n), lambda *g: (step_of(*g), 0)))
    return specs


def _cast_slabs(in_refs, out_refs):
    for w_ref, o_ref in zip(in_refs, out_refs):
        o_ref[...] = w_ref[...].astype(o_ref.dtype)


def _silu(x):
    h = 0.5 * x
    return h * jnp.tanh(h) + h


def _dot(a, b):
    return jnp.dot(a, b, preferred_element_type=F32)


def _dot_nt(a, b):
    return lax.dot_general(a, b, (((1,), (1,)), ((), ())), preferred_element_type=F32)


def _dot_tn(a, b):
    return lax.dot_general(a, b, (((0,), (0,)), ((), ())), preferred_element_type=F32)


def _split3(v):
    hi = v.astype(BF16)
    r1 = v - hi.astype(F32)
    mid = r1.astype(BF16)
    lo = (r1 - mid.astype(F32)).astype(BF16)
    return hi, mid, lo


def _dot01_left(m01, v):
    hi, mid, lo = _split3(v)
    return _dot(m01, hi) + (_dot(m01, mid) + _dot(m01, lo))


def _pack3(v):
    hi, mid, lo = _split3(v)
    packed = hi.astype(F32) + pltpu.roll(mid.astype(F32), PACK_STRIDE, 1) + pltpu.roll(lo.astype(F32), 2 * PACK_STRIDE, 1)
    return packed.astype(BF16)


def _ada_kernel(c_ref, w_ref, b_ref, o_ref):
    a = _silu(c_ref[...]).astype(BF16)
    o_ref[...] = _dot(a, w_ref[...].astype(BF16)) + b_ref[...]


def _ada(c, w, b):
    r, d = c.shape
    n = w.shape[1]
    tn = _pick(n, (512, 256, 128))
    return pl.pallas_call(
        _ada_kernel,
        grid=(n // tn,),
        in_specs=[pl.BlockSpec((r, d), lambda j: (0, 0)),
                  pl.BlockSpec((d, tn), lambda j: (0, j)),
                  pl.BlockSpec((1, tn), lambda j: (0, j))],
        out_specs=pl.BlockSpec((r, tn), lambda j: (0, j)),
        out_shape=jax.ShapeDtypeStruct((r, n), F32),
        compiler_params=_cparams("arbitrary"),
        name="ada",
    )(c, w, b.reshape(1, n))


def _norm_kernel(x_ref, g_ref, sc_ref, sh_ref, o_ref):
    x = x_ref[...]
    r = lax.rsqrt(jnp.mean(x * x, axis=-1, keepdims=True) + EPS)
    y = x * r * g_ref[...]
    o_ref[...] = (y * (1.0 + sc_ref[...]) + sh_ref[...]).astype(o_ref.dtype)


def _norm_mod(x, g, sc, sh, seq, boff):
    m, d = x.shape
    tm = _pick(seq, (512, 256, 128, 64, 32, 16, 8))
    per = seq // tm
    mod_spec = pl.BlockSpec((None, 1, d), lambda i: (boff + i // per, 0, 0))
    return pl.pallas_call(
        _norm_kernel,
        grid=(m // tm,),
        in_specs=[pl.BlockSpec((tm, d), lambda i: (i, 0)),
                  pl.BlockSpec((1, d), lambda i: (0, 0)),
                  mod_spec, mod_spec],
        out_specs=pl.BlockSpec((tm, d), lambda i: (i, 0)),
        out_shape=jax.ShapeDtypeStruct((m, d), BF16),
        compiler_params=_cparams("arbitrary"),
        name="norm_mod",
    )(x, g.reshape(1, d), sc, sh)


def _mm_in_kernel(a_ref, wa_ref, wb_ref, wdt_ref, o_ref, dt_ref, *, na):
    a = a_ref[...]
    j = pl.program_id(1)
    w = jnp.where(j < na, wa_ref[...], wb_ref[...])
    o_ref[...] = _dot(a, w).astype(o_ref.dtype)

    @pl.when(j == 0)
    def _():
        dt_ref[...] = _dot(a, wdt_ref[...])


def _mm_in(a, w_a, a_cols, w_b, w_dt, dest, tn):
    m, k = a.shape
    na, nbt = a_cols // tn, w_b.shape[1] // tn
    assert a_cols % tn == 0 and w_b.shape[1] % tn == 0 and sorted(dest) == list(range(na + nbt))
    tm = _pick(m, (1024, 512, 256, 128))

    def out_tile(j):
        r = dest[0]
        for t in range(1, na + nbt):
            r = jnp.where(j >= t, dest[t], r)
        return r

    return pl.pallas_call(
        functools.partial(_mm_in_kernel, na=na),
        grid=(m // tm, na + nbt),
        in_specs=[pl.BlockSpec((tm, k), lambda i, j: (i, 0)),
                  pl.BlockSpec((k, tn), lambda i, j: (0, jnp.minimum(j, na - 1))),
                  pl.BlockSpec((k, tn), lambda i, j: (0, jnp.maximum(j - na, 0))),
                  pl.BlockSpec((k, LANES), lambda i, j: (0, 0))],
        out_specs=[pl.BlockSpec((tm, tn), lambda i, j: (i, out_tile(j))),
                   pl.BlockSpec((tm, LANES), lambda i, j: (i, 0))],
        out_shape=[jax.ShapeDtypeStruct((m, (na + nbt) * tn), BF16),
                   jax.ShapeDtypeStruct((m, LANES), F32)],
        compiler_params=_cparams("arbitrary", "arbitrary"),
        name="mm_in",
    )(a, w_a, w_b, w_dt)


def _gate_operand(gate, tm, tn, seq, boff, nb, ij=lambda i, j: (i, j)):
    if seq % tm == 0:
        per = seq // tm
        return gate, pl.BlockSpec((None, 1, tn), lambda *g: (boff + ij(*g)[0] // per, 0, ij(*g)[1]))
    rows = jnp.repeat(gate[boff:boff + nb, 0], seq, axis=0)
    return rows, pl.BlockSpec((tm, tn), lambda *g: ij(*g))


def _mm_out_kernel(y_ref, o_ref, wa_ref, wb_ref, x_ref, g_ref, out_ref):
    acc = _dot(y_ref[...], wa_ref[...]) + _dot(o_ref[...], wb_ref[...])
    out_ref[...] = x_ref[...] + g_ref[...] * acc


def _mm_out(y, o, w, x, gate, seq, boff):
    m, wd = y.shape
    d = w.shape[1]
    tm = _pick(m, (1024, 512, 256, 128))
    tn = _pick(d, (512, 256, 128))
    gate, gate_spec = _gate_operand(gate, tm, tn, seq, boff, m // seq)
    return pl.pallas_call(
        _mm_out_kernel,
        grid=(m // tm, d // tn),
        in_specs=[pl.BlockSpec((tm, wd), lambda i, j: (i, 0)),
                  pl.BlockSpec((tm, wd), lambda i, j: (i, 0)),
                  pl.BlockSpec((wd, tn), lambda i, j: (0, j)),
                  pl.BlockSpec((wd, tn), lambda i, j: (1, j)),
                  pl.BlockSpec((tm, tn), lambda i, j: (i, j)),
                  gate_spec],
        out_specs=pl.BlockSpec((tm, tn), lambda i, j: (i, j)),
        out_shape=jax.ShapeDtypeStruct((m, d), F32),
        compiler_params=_cparams("arbitrary", "arbitrary"),
        name="mm_out",
    )(y, o, w, w, x, gate)


def _mm_gu_kernel(h_ref, wg_ref, wu_ref, *rest):
    cast_in, (o_ref, *cast_out) = rest[:len(rest) // 2], rest[len(rest) // 2:]
    h = h_ref[...]
    g = _dot(h, wg_ref[...])
    u = _dot(h, wu_ref[...])
    o_ref[...] = (_silu(g) * u).astype(o_ref.dtype)
    _cast_slabs(cast_in, cast_out)


def _mm_gate_up(h, w, cast_ws=()):
    m, d = h.shape
    ff = w.shape[1] // 2
    tm = _pick(m, (2048, 1024, 512, 256, 128))
    tn = _pick(ff, (512, 256, 128))
    nj = ff // tn
    slabs = _cast_slab_specs(cast_ws, (m // tm) * nj, lambda i, j: i * nj + j) if cast_ws else []
    if slabs is None:
        slabs, late = [], [w_.astype(BF16) for w_ in cast_ws]
    else:
        late = None
    outs = pl.pallas_call(
        _mm_gu_kernel,
        grid=(m // tm, nj),
        in_specs=[pl.BlockSpec((tm, d), lambda i, j: (i, 0)),
                  pl.BlockSpec((d, tn), lambda i, j: (0, j)),
                  pl.BlockSpec((d, tn), lambda i, j: (0, nj + j))] + slabs,
        out_specs=[pl.BlockSpec((tm, tn), lambda i, j: (i, j))] + slabs,
        out_shape=[jax.ShapeDtypeStruct((m, ff), BF16)]
        + ([jax.ShapeDtypeStruct(w_.shape, BF16) for w_ in cast_ws] if slabs else []),
        compiler_params=_cparams("arbitrary", "arbitrary"),
        name="mm_gate_up",
    )(h, w, w, *(cast_ws if slabs else ()))
    return outs[0], (late if late is not None else list(outs[1:]))


def _mm_down_kernel(a_ref, w_ref, x_ref, g_ref, o_ref):
    o_ref[...] = x_ref[...] + g_ref[...] * _dot(a_ref[...], w_ref[...])


def _mm_down(a, w, x, gate, seq, boff):
    m, f = a.shape
    d = w.shape[1]
    tm = _pick(m, (512, 256, 128))
    tn = _pick(d, (512, 256, 128))
    gate, gate_spec = _gate_operand(gate, tm, tn, seq, boff, m // seq, ij=lambda j, i: (i, j))
    return pl.pallas_call(
        _mm_down_kernel,
        grid=(d // tn, m // tm),
        in_specs=[pl.BlockSpec((tm, f), lambda j, i: (i, 0)),
                  pl.BlockSpec((f, tn), lambda j, i: (0, j)),
                  pl.BlockSpec((tm, tn), lambda j, i: (i, j)),
                  gate_spec],
        out_specs=pl.BlockSpec((tm, tn), lambda j, i: (i, j)),
        out_shape=jax.ShapeDtypeStruct((m, d), F32),
        compiler_params=_cparams("arbitrary", "arbitrary"),
        name="mm_down",
    )(a, w, x, gate)


LEFT_ROWS = 3 * BF16_ROWS
SHIFT_ROWS = SSD_CONV * CHUNK + SUBLANES


def _shift_matrix(valid):
    keep = SSD_CONV - 1
    t = np.zeros((SHIFT_ROWS, LEFT_ROWS + valid), np.float32)
    for k in range(SSD_CONV):
        for r in range(valid):
            src = r + k - keep
            if src >= 0:
                t[k * CHUNK + r, LEFT_ROWS + src] = 1.0
            else:
                for piece in range(3):
                    t[k * CHUNK + r, piece * BF16_ROWS + BF16_ROWS + src] = 1.0
    for r in range(keep):
        t[SSD_CONV * CHUNK + r, LEFT_ROWS + valid - keep + r] = 1.0
    return jnp.asarray(t, BF16)


def _ssd_stages(refs, valid, hpg, has_init):
    if has_init:
        (z_ref, x_ref, b_ref, c_ref, dt_ref, convl_ref, init_ref, tmat_ref, cw_ref, cb_ref, dtb_ref, alog_ref,
         dskip_ref, nw_ref, sel_ref, y_ref, state_ref, convs_ref, left, tmp, st) = refs
    else:
        (z_ref, x_ref, b_ref, c_ref, dt_ref, convl_ref, tmat_ref, cw_ref, cb_ref, dtb_ref, alog_ref,
         dskip_ref, nw_ref, sel_ref, y_ref, state_ref, convs_ref, left, tmp, st) = refs
        init_ref = None
    c = pl.program_id(1)
    width = x_ref.shape[1]
    heads = width // HEAD_DIM
    gn = SSD_GROUPS * SSD_STATE
    gw = hpg * HEAD_DIM
    L = CHUNK
    keep = SSD_CONV - 1

    @pl.when(c == 0)
    def _():
        tmp[...] = jnp.zeros_like(tmp)
        tmp[BF16_ROWS - keep:BF16_ROWS, :] = convl_ref[...]
        hi, mid, lo = _split3(tmp[...])
        left[0:BF16_ROWS, :] = lo
        left[BF16_ROWS:2 * BF16_ROWS, :] = mid
        left[2 * BF16_ROWS:LEFT_ROWS, :] = hi
        if has_init:
            st[...] = init_ref[...].T
        else:
            st[...] = jnp.zeros_like(st)

    tmat = tmat_ref[...]
    bounds = ((0, width), (width, width + gn), (width + gn, width + 2 * gn))
    shifted = [_dot(tmat, jnp.concatenate([left[:, lo:hi], ref[...]], axis=0))
               for ref, (lo, hi) in zip((x_ref, b_ref, c_ref), bounds)]
    yield

    def conv(sh, lo, hi):
        acc = cb_ref[:, lo:hi] + sh[0:L] * cw_ref[0:1, lo:hi]
        for k in range(1, SSD_CONV):
            acc = acc + sh[k * L:(k + 1) * L] * cw_ref[k:k + 1, lo:hi]
        convs_ref[:, lo:hi] = sh[SSD_CONV * L:SSD_CONV * L + keep]
        return _silu(acc)

    xs, bc, cc = [conv(sh, lo, hi) for sh, (lo, hi) in zip(shifted, bounds)]

    left[0:2 * BF16_ROWS, :] = jnp.zeros((2 * BF16_ROWS, left.shape[1]), BF16)
    left[2 * BF16_ROWS:LEFT_ROWS, 0:width] = x_ref[valid - BF16_ROWS:valid, :]
    left[2 * BF16_ROWS:LEFT_ROWS, width:width + gn] = b_ref[valid - BF16_ROWS:valid, :]
    left[2 * BF16_ROWS:LEFT_ROWS, width + gn:width + 2 * gn] = c_ref[valid - BF16_ROWS:valid, :]

    row = lax.broadcasted_iota(jnp.int32, (L, LANES), 0)
    lane = lax.broadcasted_iota(jnp.int32, (L, LANES), 1)
    if valid < L:
        dt_raw = jnp.concatenate([dt_ref[...], jnp.zeros((L - valid, LANES), F32)], axis=0)
    else:
        dt_raw = dt_ref[...]
    u = dt_raw + dtb_ref[...]
    dt = jnp.maximum(u, 0.0) + jnp.log1p(jnp.exp(-jnp.abs(u)))
    dt = jnp.where((row < valid) & (lane < heads), dt, 0.0)
    a = dt * (-jnp.exp(alog_ref[...]))

    tri = jnp.where(lax.broadcasted_iota(jnp.int32, (L, L), 0) >= lax.broadcasted_iota(jnp.int32, (L, L), 1),
                    1.0, 0.0).astype(BF16)
    a_cs = _dot01_left(tri, a)
    both = _dot(jnp.concatenate([_pack3(a_cs), _pack3(dt)], axis=0), sel_ref[...])
    yield
    acs_b = both[0:L]
    dt_b = both[L:2 * L]

    s_idx = lane & (HEAD_DIM - 1)
    causal = jnp.where(row >= s_idx, 0.0, -jnp.inf)
    diag = row == s_idx
    dec = []
    for j in range(width // LANES):
        blk = acs_b[:, j * LANES:(j + 1) * LANES]
        acs_row = jnp.sum(jnp.where(diag, blk, 0.0), axis=0, keepdims=True)
        dec.append(jnp.exp(blk - acs_row + causal))
    decay = jnp.concatenate(dec, axis=1)
    last_b = acs_b[L - 1:L, :]
    e_in = jnp.exp(acs_b)
    e_out = jnp.exp(last_b - acs_b)
    e_last = jnp.exp(last_b)

    xbar = xs * dt_b
    xd16 = (xbar * e_out).astype(BF16)

    pr = lax.broadcasted_iota(jnp.int32, (2 * HEAD_DIM, 2 * HEAD_DIM), 0) >= HEAD_DIM
    pc = lax.broadcasted_iota(jnp.int32, (2 * HEAD_DIM, 2 * HEAD_DIM), 1) >= HEAD_DIM
    pair_mask = pr == pc

    zg = _silu(z_ref[...].astype(F32))
    groups = range(SSD_GROUPS)
    span = lambda g: slice(g * gw, (g + 1) * gw)
    bgs = [bc[:, g * SSD_STATE:(g + 1) * SSD_STATE].astype(BF16) for g in groups]
    cgs = [cc[:, g * SSD_STATE:(g + 1) * SSD_STATE].astype(BF16) for g in groups]
    cb_ts = [_dot_nt(cgs[g], jnp.concatenate([bgs[g]] * hpg, axis=0)) for g in groups]
    st_gs = [st[:, span(g)] for g in groups]
    y_offs = [_dot(cgs[g], st_gs[g].astype(BF16)) for g in groups]
    st_new = [_dot_tn(bgs[g], xd16[:, span(g)]) for g in groups]
    yield
    for g in groups:
        st[:, span(g)] = st_gs[g] * e_last[:, span(g)] + st_new[g]
    y_diags = []
    for g in groups:
        gmat = (cb_ts[g] * decay[:, span(g)]).astype(BF16)
        for p in range(hpg // 2):
            plo = g * gw + p * 2 * HEAD_DIM
            xp = xbar[:, plo:plo + 2 * HEAD_DIM]
            bd = jnp.where(pair_mask, jnp.concatenate([xp, xp], axis=0), 0.0).astype(BF16)
            y_diags.append(_dot(gmat[:, p * 2 * HEAD_DIM:(p + 1) * 2 * HEAD_DIM], bd))
    yield
    for g in groups:
        y_diag = jnp.concatenate(y_diags[g * (hpg // 2):(g + 1) * (hpg // 2)], axis=1)
        y = y_diag + y_offs[g] * e_in[:, span(g)] + dskip_ref[:, span(g)] * xs[:, span(g)]
        gt = y[0:valid, :] * zg[:, span(g)]
        rs = lax.rsqrt(jnp.mean(gt * gt, axis=-1, keepdims=True) + EPS)
        y_ref[:, span(g)] = (gt * rs * nw_ref[:, span(g)]).astype(y_ref.dtype)

    @pl.when(c == pl.num_programs(1) - 1)
    def _():
        state_ref[...] = st[...].T


def _ssd_parts(proj, dt_raw, conv_left, init_state, lp, nb, seq, width, cols):
    m = proj.shape[0]
    valid = min(CHUNK, seq)
    nc = seq // valid
    heads = width // HEAD_DIM
    hpg = heads // SSD_GROUPS
    gn = SSD_GROUPS * SSD_STATE
    cch = width + 2 * gn
    has_init = init_state is not None
    assert valid % BF16_ROWS == 0 and hpg % 2 == 0 and heads <= PACK_STRIDE and 3 * PACK_STRIDE <= LANES

    def rows(b, c):
        return b * nc + c

    def col_spec(w, off):
        assert off % w == 0
        return pl.BlockSpec((valid, w), lambda b, c: (rows(b, c), off // w))

    full = lambda shape: pl.BlockSpec(shape, lambda b, c: tuple(0 for _ in shape))
    in_specs = [col_spec(width, cols["z"]), col_spec(width, cols["x"]),
                col_spec(gn, cols["B"]), col_spec(gn, cols["C"]),
                pl.BlockSpec((valid, LANES), lambda b, c: (rows(b, c), 0)),
                pl.BlockSpec((None, SSD_CONV - 1, cch), lambda b, c: (b, 0, 0))]
    args = [proj, proj, proj, proj, dt_raw, conv_left]
    if has_init:
        in_specs.append(pl.BlockSpec((None, width, SSD_STATE), lambda b, c: (b, 0, 0)))
        args.append(init_state)
    tmat = _shift_matrix(valid)
    in_specs += [full(tmat.shape), full((SSD_CONV, cch)), full((1, cch)), full((1, LANES)), full((1, LANES)),
                 full((1, width)), full((1, width)), full((LANES, width))]
    args += [tmat, lp["conv_w"], lp["conv_b"], lp["dt_bias"], lp["a_log"], lp["d_skip"], lp["ssd_norm_w"], lp["sel"]]
    stages = functools.partial(_ssd_stages, valid=valid, hpg=hpg, has_init=has_init)
    out_specs = [pl.BlockSpec((valid, width), lambda b, c: (rows(b, c), 0)),
                 pl.BlockSpec((None, width, SSD_STATE), lambda b, c: (b, 0, 0)),
                 pl.BlockSpec((None, SSD_CONV - 1, cch), lambda b, c: (b, 0, 0))]
    out_shapes = [jax.ShapeDtypeStruct((m, width), BF16),
                  jax.ShapeDtypeStruct((nb, width, SSD_STATE), F32),
                  jax.ShapeDtypeStruct((nb, SSD_CONV - 1, cch), F32)]
    scratch = [pltpu.VMEM((LEFT_ROWS, cch), BF16),
               pltpu.VMEM((BF16_ROWS, cch), F32),
               pltpu.VMEM((SSD_STATE, width), F32)]
    return stages, in_specs, args, out_specs, out_shapes, scratch


def _head_rms(x, w):
    return x * lax.rsqrt(jnp.mean(x * x, axis=-1, keepdims=True) + EPS) * w


def _attn_stages(refs, lq, qpk, hb, prompt):
    if prompt:
        q_ref, ko_ref, vo_ref = refs[:3]
        rest = refs[3:]
    else:
        q_ref, ko_ref, vo_ref, kp_ref, vp_ref = refs[:5]
        rest = refs[5:]
    kw_ref, ks_ref, sink_ref, dup_ref, eye_ref, mq_ref, mo_ref, ep_ref, o_ref, kn_ref, krings, vrings, bias = rest
    b = pl.program_id(0)
    c = pl.program_id(1)
    nk = WINDOW + lq
    heads = KV_HEADS * qpk
    wc = WINDOW // CHUNK
    cur = lax.rem(c, 2)
    kring = krings.at[cur]
    vring = vrings.at[cur]

    @pl.when((b == 0) & (c == 0))
    def _():
        j = lax.broadcasted_iota(jnp.int32, (nk, heads * lq), 1)
        s = lax.broadcasted_iota(jnp.int32, (nk, heads * lq), 0)
        h = lax.shift_right_logical(j, lq.bit_length() - 1)
        l_idx = j & (lq - 1)
        slope = jnp.exp2(-8.0 * (h + 1).astype(F32) / heads)
        bias[...] = -(slope * LOG2E) * jnp.abs(WINDOW + l_idx - s).astype(F32)

    if prompt:
        @pl.when(c == 0)
        def _():
            krings[...] = jnp.zeros_like(krings)
            vrings[...] = jnp.zeros_like(vrings)
    else:
        kring[0:WINDOW, :] = (kp_ref[...] * ks_ref[...]).astype(BF16)
        vring[0:WINDOW, :] = vp_ref[...].astype(BF16)

    yield
    kw = kw_ref[...]
    for g in range(KV_HEADS):
        lo, hi = g * HEAD_DIM, (g + 1) * HEAD_DIM
        kn_ref[:, lo:hi] = _head_rms(ko_ref[:, lo:hi].astype(F32), kw)
    kring[WINDOW:nk, :] = (kn_ref[...] * ks_ref[...]).astype(BF16)
    vring[WINDOW:nk, :] = vo_ref[...].astype(BF16)

    hw = hb * HEAD_DIM
    pw = hb * lq
    kdup = _dot(kring[...], dup_ref[...]).astype(BF16)
    vt = _dot_nt(eye_ref[...], vring[...]).astype(BF16)
    mask_q = mq_ref[...]
    mask_o = mo_ref[...]
    eye_blk = ep_ref[...]
    ones = jnp.ones((SUBLANES, hw), BF16)
    yield
    if prompt:
        offs = [jnp.where(c >= wc - i, 0.0, jnp.inf) for i in range(wc)]
    pairs = range(heads // hb)
    group = lambda pi: pi // (qpk // hb)
    qbs = []
    for pi in pairs:
        qp = q_ref[:, pi * hw:(pi + 1) * hw]
        qbs.append(jnp.concatenate([qp] * hb, axis=0) * mask_q)
    ssqs = [_dot_nt(ones, qb * qb)[0:1] for qb in qbs]
    raw = [_dot_nt(kdup[:, group(pi) * hw:(group(pi) + 1) * hw], qbs[pi]) for pi in pairs]
    yield
    pns = []
    for pi in pairs:
        rq = lax.rsqrt(ssqs[pi] * (1.0 / HEAD_DIM) + EPS)
        s2 = raw[pi] * rq + bias[:, pi * pw:(pi + 1) * pw]
        sink2 = sink_ref[:, pi * pw:(pi + 1) * pw]
        m_own = jnp.maximum(jnp.max(s2[WINDOW:nk], axis=0, keepdims=True), sink2)
        if prompt:
            mx = m_own
            for i in range(wc):
                band = s2[i * CHUNK:(i + 1) * CHUNK]
                mx = jnp.maximum(mx, jnp.max(band, axis=0, keepdims=True) - offs[i])
            p = jnp.concatenate([jnp.exp2(s2[i * CHUNK:(i + 1) * CHUNK] - (mx + offs[i])) for i in range(wc)]
                                + [jnp.exp2(s2[WINDOW:nk] - mx)], axis=0)
        else:
            mx = jnp.maximum(m_own, jnp.max(s2[0:WINDOW], axis=0, keepdims=True))
            p = jnp.exp2(s2 - mx)
        den = jnp.sum(p, axis=0, keepdims=True) + jnp.exp2(sink2 - mx)
        pns.append((p * (1.0 / den)).astype(BF16))
    ots = [_dot(vt[group(pi) * HEAD_DIM:(group(pi) + 1) * HEAD_DIM, :], pns[pi]).astype(BF16)
           for pi in pairs]
    yield
    obs = [jnp.concatenate([ot] * hb, axis=0) * mask_o for ot in ots]
    outs = [_dot_nt(eye_blk, ob) for ob in obs]
    for pi in pairs:
        o_ref[:, pi * hw:(pi + 1) * hw] = outs[pi].astype(o_ref.dtype)

    if prompt:
        krings.at[1 - cur][0:WINDOW, :] = kring[lq:nk, :]
        vrings.at[1 - cur][0:WINDOW, :] = vring[lq:nk, :]


def _attn_parts(proj, cols, cache_kv, lp, nb, seq, width):
    m = nb * seq
    lq = min(CHUNK, seq)
    nc = seq // lq
    heads = width // HEAD_DIM
    qpk = heads // KV_HEADS
    kvw = KV_HEADS * HEAD_DIM
    prompt = cache_kv is None
    nk = WINDOW + lq
    assert qpk % 2 == 0 and lq & (lq - 1) == 0 and lq % BF16_ROWS == 0 and WINDOW % CHUNK == 0
    assert cols["q"] % width == 0 and cols["k"] % kvw == 0 and cols["v"] % kvw == 0
    assert prompt and lq == CHUNK or not prompt and nc == 1
    row_spec = lambda w, off: pl.BlockSpec((lq, w), lambda b, c: (b * nc + c, off // w))
    full = lambda shape: pl.BlockSpec(shape, lambda b, c: tuple(0 for _ in shape))
    in_specs = [row_spec(width, cols["q"]), row_spec(kvw, cols["k"]), row_spec(kvw, cols["v"])]
    args = [proj, proj, proj]
    if not prompt:
        in_specs += [pl.BlockSpec((WINDOW, kvw), lambda b, c: (b, 0))] * 2
        args += list(cache_kv)
    sink2 = (jnp.repeat(lp["sinks"], lq) * LOG2E).reshape(1, heads * lq)
    hb = MXU_WIDTH // HEAD_DIM if qpk % (MXU_WIDTH // HEAD_DIM) == 0 else 2
    hw = hb * HEAD_DIM
    kv_idx = np.arange(kvw)
    dup_idx = np.arange(KV_HEADS * hw)
    dup = (kv_idx[:, None] // HEAD_DIM == dup_idx[None, :] // hw) & (kv_idx[:, None] % HEAD_DIM == dup_idx[None, :] % HEAD_DIM)
    ql = np.arange(hb * lq)
    dl = np.arange(hw)
    mask_q = ql[:, None] // lq == dl[None, :] // HEAD_DIM
    eye_blk = np.arange(lq)[:, None] == ql[None, :] % lq
    consts = [jnp.asarray(a, BF16) for a in (dup, np.eye(kvw), mask_q, mask_q.T, eye_blk)]
    in_specs += [full((1, HEAD_DIM)), full((1, kvw)), full((1, heads * lq))] + [full(a.shape) for a in consts]
    args += [lp["k_norm_w"], lp["k_scale"], sink2] + consts
    stages = functools.partial(_attn_stages, lq=lq, qpk=qpk, hb=hb, prompt=prompt)
    out_specs = [pl.BlockSpec((lq, width), lambda b, c: (b * nc + c, 0)),
                 pl.BlockSpec((lq, kvw), lambda b, c: (b * nc + c, 0))]
    out_shapes = [jax.ShapeDtypeStruct((m, width), BF16),
                  jax.ShapeDtypeStruct((m, kvw), F32)]
    scratch = [pltpu.VMEM((2, nk, kvw), BF16),
               pltpu.VMEM((2, nk, kvw), BF16),
               pltpu.VMEM((nk, heads * lq), F32)]
    return stages, in_specs, args, out_specs, out_shapes, scratch


def _mixer_kernel(*refs, ssd_stages, attn_stages, n_ssd, n_attn, n_cast):
    (si, so, ss), (ai, ao, as_) = n_ssd, n_attn
    n_in, n_out = si + ai + n_cast, so + ao + n_cast
    ins, outs, scr = refs[:n_in], refs[n_in:n_in + n_out], refs[n_in + n_out:]
    ssd = ssd_stages(ins[:si] + outs[:so] + scr[:ss])
    attn = attn_stages(ins[si:si + ai] + outs[so:so + ao] + scr[ss:])
    _cast_slabs(ins[si + ai:], outs[so + ao:])
    for gen in (attn, ssd, attn, attn, ssd, ssd, attn, ssd, attn, ssd):
        next(gen, None)
    for gen in (ssd, attn):
        assert next(gen, "done") == "done"


def _mixer(proj, dt_raw, conv_left, init_state, cache_kv, lp, nb, seq, width, cols, cast_ws=()):
    s_st, s_in, s_args, s_out, s_shape, s_scr = _ssd_parts(proj, dt_raw, conv_left, init_state, lp, nb, seq, width, cols)
    a_st, a_in, a_args, a_out, a_shape, a_scr = _attn_parts(proj, cols, cache_kv, lp, nb, seq, width)
    nc = seq // min(CHUNK, seq)
    slabs = _cast_slab_specs(cast_ws, nb * nc, lambda b, c: b * nc + c) if cast_ws else []
    if slabs is None:
        slabs, late = [], [w_.astype(BF16) for w_ in cast_ws]
    else:
        late = None
    cast_args = list(cast_ws) if slabs else []
    kern = functools.partial(_mixer_kernel, ssd_stages=s_st, attn_stages=a_st, n_cast=len(slabs),
                             n_ssd=(len(s_in), len(s_out), len(s_scr)), n_attn=(len(a_in), len(a_out), len(a_scr)))
    outs = pl.pallas_call(
        kern,
        grid=(nb, nc),
        in_specs=s_in + a_in + slabs,
        out_specs=s_out + a_out + slabs,
        out_shape=s_shape + a_shape + [jax.ShapeDtypeStruct(w_.shape, BF16) for w_ in cast_args],
        scratch_shapes=s_scr + a_scr,
        compiler_params=_cparams("arbitrary", "arbitrary"),
        name="mixer",
    )(*s_args, *a_args, *cast_args)
    n_main = len(s_out) + len(a_out)
    return outs[:n_main], (late if late is not None else list(outs[n_main:]))


def _prep_layer(l, w_ada, b_ada, g_mix, w_in, conv_w, conv_b, dt_bias, a_log, d_skip, ssd_norm_w,
                q_norm_w, k_norm_w, sinks, w_out, g_ffn, w_gate_up, w_down):
    d = w_in.shape[1]
    width = d // 2
    gn = SSD_GROUPS * SSD_STATE
    kvw = KV_HEADS * HEAD_DIM
    heads = width // HEAD_DIM
    wi = w_in[l]
    o1 = width
    o2 = o1 + width + 2 * gn
    o3 = o2 + heads
    w_all = wi.astype(BF16)
    w_b = w_all[:, o3:]
    w_dt = jnp.pad(w_all[:, o2:o3], ((0, 0), (0, LANES - heads)))
    cols = {"z": 0, "x": width, "q": 2 * width, "B": 3 * width, "C": 3 * width + gn,
            "k": 3 * width + 2 * gn, "v": 3 * width + 2 * gn + kvw}
    tn = 2 * kvw
    assert width % tn == 0 and gn % tn == 0
    src_order = ["z", "x", "B", "C", "q", "k"]
    src_width = {"z": width, "x": width, "B": gn, "C": gn, "q": width, "k": 2 * kvw}
    dest = [cols[name] // tn + t for name in src_order for t in range(src_width[name] // tn)]
    lane_piece = jnp.arange(LANES)
    head_of_lane = jnp.arange(width) // HEAD_DIM
    sel = ((lane_piece[:, None] % PACK_STRIDE == head_of_lane[None, :])
           & (lane_piece[:, None] < 3 * PACK_STRIDE)).astype(BF16)
    pad_h = (0, LANES - heads)
    k_scale = jnp.tile(q_norm_w[l], KV_HEADS) * (HEAD_DIM ** -0.5 * LOG2E)
    return dict(
        w_ada=w_ada[l], b_ada=b_ada[l], g_mix=g_mix[l], g_ffn=g_ffn[l],
        w_a=w_all, a_cols=o2, w_b=w_b, w_dt=w_dt, cols=cols, dest=dest, tn_in=tn,
        w_out=w_out[l], w_gu=w_gate_up[l], w_dn=w_down[l],
        conv_w=conv_w[l], conv_b=conv_b[l].reshape(1, -1),
        dt_bias=jnp.pad(dt_bias[l], pad_h).reshape(1, LANES), a_log=jnp.pad(a_log[l], pad_h).reshape(1, LANES),
        d_skip=jnp.repeat(d_skip[l], HEAD_DIM).reshape(1, width), ssd_norm_w=ssd_norm_w[l].reshape(1, width),
        sel=sel, k_norm_w=k_norm_w[l].reshape(1, HEAD_DIM), k_scale=k_scale.reshape(1, kvw),
        sinks=sinks[l], width=width)


def _layer(x, mods, boff, lp, ssd_init, conv_left, cache_kv, w16=None):
    nb, seq, d = x.shape
    m = nb * seq
    width = lp["width"]
    cols = lp["cols"]
    kvw = KV_HEADS * HEAD_DIM
    sh1, sc1, g1, sh2, sc2, g2 = mods
    x2 = x.reshape(m, d)
    h = _norm_mod(x2, lp["g_mix"], sc1, sh1, seq, boff)
    proj, dt_raw = _mm_in(h, lp["w_a"], lp["a_cols"], lp["w_b"], lp["w_dt"], lp["dest"], lp["tn_in"])
    if cache_kv is not None:
        cache_kv = tuple(t.reshape(nb * WINDOW, kvw) for t in cache_kv)
    make = w16 is None
    (y, ssd_state, conv_state, o, kn), made = _mixer(proj, dt_raw, conv_left, ssd_init, cache_kv, lp, nb, seq, width,
                                                      cols, (lp["w_out"], lp["w_gu"]) if make else ())
    w_out16, w_gu16 = made if make else (w16["w_out"], w16["w_gu"])
    x1 = _mm_out(y, o, w_out16, x2, g1, seq, boff)
    h2 = _norm_mod(x1, lp["g_ffn"], sc2, sh2, seq, boff)
    act, made = _mm_gate_up(h2, w_gu16, (lp["w_dn"],) if make else ())
    w_dn16 = made[0] if make else w16["w_dn"]
    out = _mm_down(act, w_dn16, x1, g2, seq, boff)
    keep = min(WINDOW, seq) if cache_kv is None else seq
    heads = width // HEAD_DIM
    k_state = kn.reshape(nb, seq, KV_HEADS, HEAD_DIM)[:, seq - keep:]
    v_state = proj[:, cols["v"]:cols["v"] + kvw].astype(F32).reshape(nb, seq, KV_HEADS, HEAD_DIM)[:, seq - keep:]
    ssd_state = ssd_state.reshape(nb, heads, HEAD_DIM, SSD_STATE)
    w16 = dict(w_out=w_out16, w_gu=w_gu16, w_dn=w_dn16)
    return out.reshape(nb, seq, d), ssd_state, conv_state, k_state, v_state, w16


def kernel(x_prompt, x_sample, state_ssd, state_conv, cache_k, cache_v, c_prompt, c_sample, w_ada, b_ada, g_mix, w_in, conv_w, conv_b, dt_bias, a_log, d_skip, ssd_norm_w, q_norm_w, k_norm_w, sinks, w_out, g_ffn, w_gate_up, w_down):
    depth = w_in.shape[0]
    bp, _, d = x_prompt.shape
    bs = x_sample.shape[0]
    width = d // 2
    assert cache_k.shape[2] == WINDOW and x_sample.shape[1] <= CHUNK
    yp, ys = x_prompt, x_sample
    outs = [[] for _ in range(8)]
    c_all = jnp.concatenate([c_prompt, c_sample], axis=0)
    zero_conv = jnp.zeros((bp, SSD_CONV - 1, width + 2 * SSD_GROUPS * SSD_STATE), F32)
    for l in range(depth):
        lp = _prep_layer(l, w_ada, b_ada, g_mix, w_in, conv_w, conv_b, dt_bias, a_log, d_skip, ssd_norm_w,
                         q_norm_w, k_norm_w, sinks, w_out, g_ffn, w_gate_up, w_down)
        mod = _ada(c_all, lp["w_ada"], lp["b_ada"]).reshape(bp + bs, N_MOD, 1, d)
        mods = [mod[:, i] for i in range(N_MOD)]
        yp, s1, s2, s3, s4, w16 = _layer(yp, mods, 0, lp, None, zero_conv, None)
        init = state_ssd[l].reshape(bs, width, SSD_STATE)
        ys, t1, t2, t3, t4, _ = _layer(ys, mods, bp, lp, init, state_conv[l], (cache_k[l], cache_v[l]), w16)
        for lst, v in zip(outs, (s1, s2, s3, s4, t1, t2, t3, t4)):
            lst.append(v)
    stacked = [jnp.stack(v, axis=0) for v in outs]
    return (yp, ys, *stacked)
```

```python
import functools
import math

import numpy as np
import jax
import jax.numpy as jnp
from jax import lax
from jax.experimental import pallas as pl
from jax.experimental.pallas import tpu as pltpu

CHUNK = 64
HEAD_DIM = 64
SSD_GROUPS = 4
SSD_STATE = 128
SSD_CONV = 4
KV_HEADS = 4
WINDOW = 128
N_MOD = 6
EPS = 1e-6
LOG2E = math.log2(math.e)

LANES = 128
SUBLANES = 8
BF16_ROWS = 16
MXU_WIDTH = 256
VMEM_LIMIT_BYTES = 56 * 1024 * 1024

PACK_STRIDE = 32

F32 = jnp.float32
BF16 = jnp.bfloat16


def _cparams(*sem):
    return pltpu.CompilerParams(dimension_semantics=sem, vmem_limit_bytes=VMEM_LIMIT_BYTES)


def _pick(n, candidates):
    for c in candidates:
        if c <= n and n % c == 0:
            return c
    return n


def _cast_slab_specs(ws, steps, step_of):
    specs = []
    for w in ws:
        k, n = w.shape
        if k % steps or (k // steps) % BF16_ROWS:
            return None
        specs.append(pl.BlockSpec((k // steps, n), lambda *g: (step_of(*g), 0)))
    return specs


def _cast_slabs(in_refs, out_refs):
    for w_ref, o_ref in zip(in_refs, out_refs):
        o_ref[...] = w_ref[...].astype(o_ref.dtype)


def _silu(x):
    h = 0.5 * x
    return h * jnp.tanh(h) + h


def _dot(a, b):
    return jnp.dot(a, b, preferred_element_type=F32)


def _dot_nt(a, b):
    return lax.dot_general(a, b, (((1,), (1,)), ((), ())), preferred_element_type=F32)


def _dot_tn(a, b):
    return lax.dot_general(a, b, (((0,), (0,)), ((), ())), preferred_element_type=F32)


def _split3(v):
    hi = v.astype(BF16)
    r1 = v - hi.astype(F32)
    mid = r1.astype(BF16)
    lo = (r1 - mid.astype(F32)).astype(BF16)
    return hi, mid, lo


def _dot01_left(m01, v):
    hi, mid, lo = _split3(v)
    return _dot(m01, hi) + (_dot(m01, mid) + _dot(m01, lo))


def _pack3(v):
    hi, mid, lo = _split3(v)
    packed = hi.astype(F32) + pltpu.roll(mid.astype(F32), PACK_STRIDE, 1) + pltpu.roll(lo.astype(F32), 2 * PACK_STRIDE, 1)
    return packed.astype(BF16)


def _ada_kernel(c_ref, w_ref, b_ref, o_ref):
    a = _silu(c_ref[...]).astype(BF16)
    o_ref[...] = _dot(a, w_ref[...].astype(BF16)) + b_ref[...]


def _ada(c, w, b):
    r, d = c.shape
    n = w.shape[1]
    tn = _pick(n, (512, 256, 128))
    return pl.pallas_call(
        _ada_kernel,
        grid=(n // tn,),
        in_specs=[pl.BlockSpec((r, d), lambda j: (0, 0)),
                  pl.BlockSpec((d, tn), lambda j: (0, j)),
                  pl.BlockSpec((1, tn), lambda j: (0, j))],
        out_specs=pl.BlockSpec((r, tn), lambda j: (0, j)),
        out_shape=jax.ShapeDtypeStruct((r, n), F32),
        compiler_params=_cparams("arbitrary"),
        name="ada",
    )(c, w, b.reshape(1, n))


def _norm_kernel(x_ref, g_ref, sc_ref, sh_ref, o_ref):
    x = x_ref[...]
    r = lax.rsqrt(jnp.mean(x * x, axis=-1, keepdims=True) + EPS)
    y = x * r * g_ref[...]
    o_ref[...] = (y * (1.0 + sc_ref[...]) + sh_ref[...]).astype(o_ref.dtype)


def _norm_mod(x, g, sc, sh, seq, boff):
    m, d = x.shape
    tm = _pick(seq, (512, 256, 128, 64, 32, 16, 8))
    per = seq // tm
    mod_spec = pl.BlockSpec((None, 1, d), lambda i: (boff + i // per, 0, 0))
    return pl.pallas_call(
        _norm_kernel,
        grid=(m // tm,),
        in_specs=[pl.BlockSpec((tm, d), lambda i: (i, 0)),
                  pl.BlockSpec((1, d), lambda i: (0, 0)),
                  mod_spec, mod_spec],
        out_specs=pl.BlockSpec((tm, d), lambda i: (i, 0)),
        out_shape=jax.ShapeDtypeStruct((m, d), BF16),
        compiler_params=_cparams("arbitrary"),
        name="norm_mod",
    )(x, g.reshape(1, d), sc, sh)


def _mm_in_kernel(a_ref, wa_ref, wb_ref, wdt_ref, o_ref, dt_ref, *, na):
    a = a_ref[...]
    j = pl.program_id(1)
    w = jnp.where(j < na, wa_ref[...], wb_ref[...])
    o_ref[...] = _dot(a, w).astype(o_ref.dtype)

    @pl.when(j == 0)
    def _():
        dt_ref[...] = _dot(a, wdt_ref[...])


def _mm_in(a, w_a, a_cols, w_b, w_dt, dest, tn):
    m, k = a.shape
    na, nbt = a_cols // tn, w_b.shape[1] // tn
    assert a_cols % tn == 0 and w_b.shape[1] % tn == 0 and sorted(dest) == list(range(na + nbt))
    tm = _pick(m, (1024, 512, 256, 128))

    def out_tile(j):
        r = dest[0]
        for t in range(1, na + nbt):
            r = jnp.where(j >= t, dest[t], r)
        return r

    return pl.pallas_call(
        functools.partial(_mm_in_kernel, na=na),
        grid=(m // tm, na + nbt),
        in_specs=[pl.BlockSpec((tm, k), lambda i, j: (i, 0)),
                  pl.BlockSpec((k, tn), lambda i, j: (0, jnp.minimum(j, na - 1))),
                  pl.BlockSpec((k, tn), lambda i, j: (0, jnp.maximum(j - na, 0))),
                  pl.BlockSpec((k, LANES), lambda i, j: (0, 0))],
        out_specs=[pl.BlockSpec((tm, tn), lambda i, j: (i, out_tile(j))),
                   pl.BlockSpec((tm, LANES), lambda i, j: (i, 0))],
        out_shape=[jax.ShapeDtypeStruct((m, (na + nbt) * tn), BF16),
                   jax.ShapeDtypeStruct((m, LANES), F32)],
        compiler_params=_cparams("arbitrary", "arbitrary"),
        name="mm_in",
    )(a, w_a, w_b, w_dt)


def _gate_operand(gate, tm, tn, seq, boff, nb, ij=lambda i, j: (i, j)):
    if seq % tm == 0:
        per = seq // tm
        return gate, pl.BlockSpec((None, 1, tn), lambda *g: (boff + ij(*g)[0] // per, 0, ij(*g)[1]))
    rows = jnp.repeat(gate[boff:boff + nb, 0], seq, axis=0)
    return rows, pl.BlockSpec((tm, tn), lambda *g: ij(*g))


def _mm_out_kernel(y_ref, o_ref, wa_ref, wb_ref, x_ref, g_ref, out_ref):
    acc = _dot(y_ref[...], wa_ref[...]) + _dot(o_ref[...], wb_ref[...])
    out_ref[...] = x_ref[...] + g_ref[...] * acc


def _mm_out(y, o, w, x, gate, seq, boff):
    m, wd = y.shape
    d = w.shape[1]
    tm = _pick(m, (1024, 512, 256, 128))
    tn = _pick(d, (1024, 512, 256, 128))
    gate, gate_spec = _gate_operand(gate, tm, tn, seq, boff, m // seq)
    return pl.pallas_call(
        _mm_out_kernel,
        grid=(m // tm, d // tn),
        in_specs=[pl.BlockSpec((tm, wd), lambda i, j: (i, 0)),
                  pl.BlockSpec((tm, wd), lambda i, j: (i, 0)),
                  pl.BlockSpec((wd, tn), lambda i, j: (0, j)),
                  pl.BlockSpec((wd, tn), lambda i, j: (1, j)),
                  pl.BlockSpec((tm, tn), lambda i, j: (i, j)),
                  gate_spec],
        out_specs=pl.BlockSpec((tm, tn), lambda i, j: (i, j)),
        out_shape=jax.ShapeDtypeStruct((m, d), F32),
        compiler_params=_cparams("arbitrary", "arbitrary"),
        name="mm_out",
    )(y, o, w, w, x, gate)


def _mm_gu_kernel(h_ref, wg_ref, wu_ref, *rest):
    cast_in, (o_ref, *cast_out) = rest[:len(rest) // 2], rest[len(rest) // 2:]
    h = h_ref[...]
    g = _dot(h, wg_ref[...])
    u = _dot(h, wu_ref[...])
    o_ref[...] = (_silu(g) * u).astype(o_ref.dtype)
    _cast_slabs(cast_in, cast_out)


def _mm_gate_up(h, w, cast_ws=()):
    m, d = h.shape
    ff = w.shape[1] // 2
    tm = _pick(m, (2048, 1024, 512, 256, 128))
    tn = _pick(ff, (512, 256, 128))
    nj = ff // tn
    slabs = _cast_slab_specs(cast_ws, (m // tm) * nj, lambda i, j: i * nj + j) if cast_ws else []
    if slabs is None:
        slabs, late = [], [w_.astype(BF16) for w_ in cast_ws]
    else:
        late = None
    outs = pl.pallas_call(
        _mm_gu_kernel,
        grid=(m // tm, nj),
        in_specs=[pl.BlockSpec((tm, d), lambda i, j: (i, 0)),
                  pl.BlockSpec((d, tn), lambda i, j: (0, j)),
                  pl.BlockSpec((d, tn), lambda i, j: (0, nj + j))] + slabs,
        out_specs=[pl.BlockSpec((tm, tn), lambda i, j: (i, j))] + slabs,
        out_shape=[jax.ShapeDtypeStruct((m, ff), BF16)]
        + ([jax.ShapeDtypeStruct(w_.shape, BF16) for w_ in cast_ws] if slabs else []),
        compiler_params=_cparams("arbitrary", "arbitrary"),
        name="mm_gate_up",
    )(h, w, w, *(cast_ws if slabs else ()))
    return outs[0], (late if late is not None else list(outs[1:]))


def _mm_down_kernel(a_ref, w_ref, x_ref, g_ref, o_ref):
    o_ref[...] = x_ref[...] + g_ref[...] * _dot(a_ref[...], w_ref[...])


def _mm_down(a, w, x, gate, seq, boff):
    m, f = a.shape
    d = w.shape[1]
    tm = _pick(m, (512, 256, 128))
    tn = _pick(d, (512, 256, 128))
    gate, gate_spec = _gate_operand(gate, tm, tn, seq, boff, m // seq, ij=lambda j, i: (i, j))
    return pl.pallas_call(
        _mm_down_kernel,
        grid=(d // tn, m // tm),
        in_specs=[pl.BlockSpec((tm, f), lambda j, i: (i, 0)),
                  pl.BlockSpec((f, tn), lambda j, i: (0, j)),
                  pl.BlockSpec((tm, tn), lambda j, i: (i, j)),
                  gate_spec],
        out_specs=pl.BlockSpec((tm, tn), lambda j, i: (i, j)),
        out_shape=jax.ShapeDtypeStruct((m, d), F32),
        compiler_params=_cparams("arbitrary", "arbitrary"),
        name="mm_down",
    )(a, w, x, gate)


LEFT_ROWS = 3 * BF16_ROWS
SHIFT_ROWS = SSD_CONV * CHUNK + SUBLANES


def _shift_matrix(valid):
    keep = SSD_CONV - 1
    t = np.zeros((SHIFT_ROWS, LEFT_ROWS + valid), np.float32)
    for k in range(SSD_CONV):
        for r in range(valid):
            src = r + k - keep
            if src >= 0:
                t[k * CHUNK + r, LEFT_ROWS + src] = 1.0
            else:
                for piece in range(3):
                    t[k * CHUNK + r, piece * BF16_ROWS + BF16_ROWS + src] = 1.0
    for r in range(keep):
        t[SSD_CONV * CHUNK + r, LEFT_ROWS + valid - keep + r] = 1.0
    return jnp.asarray(t, BF16)


def _ssd_stages(refs, valid, hpg, has_init):
    if has_init:
        (z_ref, x_ref, b_ref, c_ref, dt_ref, convl_ref, init_ref, tmat_ref, cw_ref, cb_ref, dtb_ref, alog_ref,
         dskip_ref, nw_ref, sel_ref, y_ref, state_ref, convs_ref, left, tmp, st) = refs
    else:
        (z_ref, x_ref, b_ref, c_ref, dt_ref, convl_ref, tmat_ref, cw_ref, cb_ref, dtb_ref, alog_ref,
         dskip_ref, nw_ref, sel_ref, y_ref, state_ref, convs_ref, left, tmp, st) = refs
        init_ref = None
    c = pl.program_id(1)
    width = x_ref.shape[1]
    heads = width // HEAD_DIM
    gn = SSD_GROUPS * SSD_STATE
    gw = hpg * HEAD_DIM
    L = CHUNK
    keep = SSD_CONV - 1

    @pl.when(c == 0)
    def _():
        tmp[...] = jnp.zeros_like(tmp)
        tmp[BF16_ROWS - keep:BF16_ROWS, :] = convl_ref[...]
        hi, mid, lo = _split3(tmp[...])
        left[0:BF16_ROWS, :] = lo
        left[BF16_ROWS:2 * BF16_ROWS, :] = mid
        left[2 * BF16_ROWS:LEFT_ROWS, :] = hi
        if has_init:
            st[...] = init_ref[...].T
        else:
            st[...] = jnp.zeros_like(st)

    tmat = tmat_ref[...]
    bounds = ((0, width), (width, width + gn), (width + gn, width + 2 * gn))
    shifted = [_dot(tmat, jnp.concatenate([left[:, lo:hi], ref[...]], axis=0))
               for ref, (lo, hi) in zip((x_ref, b_ref, c_ref), bounds)]
    yield

    def conv(sh, lo, hi):
        acc = cb_ref[:, lo:hi] + sh[0:L] * cw_ref[0:1, lo:hi]
        for k in range(1, SSD_CONV):
            acc = acc + sh[k * L:(k + 1) * L] * cw_ref[k:k + 1, lo:hi]
        convs_ref[:, lo:hi] = sh[SSD_CONV * L:SSD_CONV * L + keep]
        return _silu(acc)

    xs, bc, cc = [conv(sh, lo, hi) for sh, (lo, hi) in zip(shifted, bounds)]

    left[0:2 * BF16_ROWS, :] = jnp.zeros((2 * BF16_ROWS, left.shape[1]), BF16)
    left[2 * BF16_ROWS:LEFT_ROWS, 0:width] = x_ref[valid - BF16_ROWS:valid, :]
    left[2 * BF16_ROWS:LEFT_ROWS, width:width + gn] = b_ref[valid - BF16_ROWS:valid, :]
    left[2 * BF16_ROWS:LEFT_ROWS, width + gn:width + 2 * gn] = c_ref[valid - BF16_ROWS:valid, :]

    row = lax.broadcasted_iota(jnp.int32, (L, LANES), 0)
    lane = lax.broadcasted_iota(jnp.int32, (L, LANES), 1)
    if valid < L:
        dt_raw = jnp.concatenate([dt_ref[...], jnp.zeros((L - valid, LANES), F32)], axis=0)
    else:
        dt_raw = dt_ref[...]
    u = dt_raw + dtb_ref[...]
    dt = jnp.maximum(u, 0.0) + jnp.log1p(jnp.exp(-jnp.abs(u)))
    dt = jnp.where((row < valid) & (lane < heads), dt, 0.0)
    a = dt * (-jnp.exp(alog_ref[...]))

    tri = jnp.where(lax.broadcasted_iota(jnp.int32, (L, L), 0) >= lax.broadcasted_iota(jnp.int32, (L, L), 1),
                    1.0, 0.0).astype(BF16)
    a_cs = _dot01_left(tri, a)
    both = _dot(jnp.concatenate([_pack3(a_cs), _pack3(dt)], axis=0), sel_ref[...])
    yield
    acs_b = both[0:L]
    dt_b = both[L:2 * L]

    s_idx = lane & (HEAD_DIM - 1)
    causal = jnp.where(row >= s_idx, 0.0, -jnp.inf)
    diag = row == s_idx
    dec = []
    for j in range(width // LANES):
        blk = acs_b[:, j * LANES:(j + 1) * LANES]
        acs_row = jnp.sum(jnp.where(diag, blk, 0.0), axis=0, keepdims=True)
        dec.append(jnp.exp(blk - acs_row + causal))
    decay = jnp.concatenate(dec, axis=1)
    last_b = acs_b[L - 1:L, :]
    e_in = jnp.exp(acs_b)
    e_out = jnp.exp(last_b - acs_b)
    e_last = jnp.exp(last_b)

    xbar = xs * dt_b
    xd16 = (xbar * e_out).astype(BF16)

    pr = lax.broadcasted_iota(jnp.int32, (2 * HEAD_DIM, 2 * HEAD_DIM), 0) >= HEAD_DIM
    pc = lax.broadcasted_iota(jnp.int32, (2 * HEAD_DIM, 2 * HEAD_DIM), 1) >= HEAD_DIM
    pair_mask = pr == pc

    zg = _silu(z_ref[...].astype(F32))
    groups = range(SSD_GROUPS)
    span = lambda g: slice(g * gw, (g + 1) * gw)
    bgs = [bc[:, g * SSD_STATE:(g + 1) * SSD_STATE].astype(BF16) for g in groups]
    cgs = [cc[:, g * SSD_STATE:(g + 1) * SSD_STATE].astype(BF16) for g in groups]
    cb_ts = [_dot_nt(cgs[g], jnp.concatenate([bgs[g]] * hpg, axis=0)) for g in groups]
    st_gs = [st[:, span(g)] for g in groups]
    y_offs = [_dot(cgs[g], st_gs[g].astype(BF16)) for g in groups]
    st_new = [_dot_tn(bgs[g], xd16[:, span(g)]) for g in groups]
    yield
    for g in groups:
        st[:, span(g)] = st_gs[g] * e_last[:, span(g)] + st_new[g]
    y_diags = []
    for g in groups:
        gmat = (cb_ts[g] * decay[:, span(g)]).astype(BF16)
        for p in range(hpg // 2):
            plo = g * gw + p * 2 * HEAD_DIM
            xp = xbar[:, plo:plo + 2 * HEAD_DIM]
            bd = jnp.where(pair_mask, jnp.concatenate([xp, xp], axis=0), 0.0).astype(BF16)
            y_diags.append(_dot(gmat[:, p * 2 * HEAD_DIM:(p + 1) * 2 * HEAD_DIM], bd))
    yield
    for g in groups:
        y_diag = jnp.concatenate(y_diags[g * (hpg // 2):(g + 1) * (hpg // 2)], axis=1)
        y = y_diag + y_offs[g] * e_in[:, span(g)] + dskip_ref[:, span(g)] * xs[:, span(g)]
        gt = y[0:valid, :] * zg[:, span(g)]
        rs = lax.rsqrt(jnp.mean(gt * gt, axis=-1, keepdims=True) + EPS)
        y_ref[:, span(g)] = (gt * rs * nw_ref[:, span(g)]).astype(y_ref.dtype)

    @pl.when(c == pl.num_programs(1) - 1)
    def _():
        state_ref[...] = st[...].T


def _ssd_parts(proj, dt_raw, conv_left, init_state, lp, nb, seq, width, cols):
    m = proj.shape[0]
    valid = min(CHUNK, seq)
    nc = seq // valid
    heads = width // HEAD_DIM
    hpg = heads // SSD_GROUPS
    gn = SSD_GROUPS * SSD_STATE
    cch = width + 2 * gn
    has_init = init_state is not None
    assert valid % BF16_ROWS == 0 and hpg % 2 == 0 and heads <= PACK_STRIDE and 3 * PACK_STRIDE <= LANES

    def rows(b, c):
        return b * nc + c

    def col_spec(w, off):
        assert off % w == 0
        return pl.BlockSpec((valid, w), lambda b, c: (rows(b, c), off // w))

    full = lambda shape: pl.BlockSpec(shape, lambda b, c: tuple(0 for _ in shape))
    in_specs = [col_spec(width, cols["z"]), col_spec(width, cols["x"]),
                col_spec(gn, cols["B"]), col_spec(gn, cols["C"]),
                pl.BlockSpec((valid, LANES), lambda b, c: (rows(b, c), 0)),
                pl.BlockSpec((None, SSD_CONV - 1, cch), lambda b, c: (b, 0, 0))]
    args = [proj, proj, proj, proj, dt_raw, conv_left]
    if has_init:
        in_specs.append(pl.BlockSpec((None, width, SSD_STATE), lambda b, c: (b, 0, 0)))
        args.append(init_state)
    tmat = _shift_matrix(valid)
    in_specs += [full(tmat.shape), full((SSD_CONV, cch)), full((1, cch)), full((1, LANES)), full((1, LANES)),
                 full((1, width)), full((1, width)), full((LANES, width))]
    args += [tmat, lp["conv_w"], lp["conv_b"], lp["dt_bias"], lp["a_log"], lp["d_skip"], lp["ssd_norm_w"], lp["sel"]]
    stages = functools.partial(_ssd_stages, valid=valid, hpg=hpg, has_init=has_init)
    out_specs = [pl.BlockSpec((valid, width), lambda b, c: (rows(b, c), 0)),
                 pl.BlockSpec((None, width, SSD_STATE), lambda b, c: (b, 0, 0)),
                 pl.BlockSpec((None, SSD_CONV - 1, cch), lambda b, c: (b, 0, 0))]
    out_shapes = [jax.ShapeDtypeStruct((m, width), BF16),
                  jax.ShapeDtypeStruct((nb, width, SSD_STATE), F32),
                  jax.ShapeDtypeStruct((nb, SSD_CONV - 1, cch), F32)]
    scratch = [pltpu.VMEM((LEFT_ROWS, cch), BF16),
               pltpu.VMEM((BF16_ROWS, cch), F32),
               pltpu.VMEM((SSD_STATE, width), F32)]
    return stages, in_specs, args, out_specs, out_shapes, scratch


def _head_rms(x, w):
    return x * lax.rsqrt(jnp.mean(x * x, axis=-1, keepdims=True) + EPS) * w


def _attn_stages(refs, lq, qpk, hb, prompt):
    if prompt:
        q_ref, ko_ref, vo_ref = refs[:3]
        rest = refs[3:]
    else:
        q_ref, ko_ref, vo_ref, kp_ref, vp_ref = refs[:5]
        rest = refs[5:]
    kw_ref, ks_ref, sink_ref, dup_ref, eye_ref, mq_ref, mo_ref, ep_ref, o_ref, kn_ref, krings, vrings, bias = rest
    b = pl.program_id(0)
    c = pl.program_id(1)
    nk = WINDOW + lq
    heads = KV_HEADS * qpk
    wc = WINDOW // CHUNK
    cur = lax.rem(c, 2)
    kring = krings.at[cur]
    vring = vrings.at[cur]

    @pl.when((b == 0) & (c == 0))
    def _():
        j = lax.broadcasted_iota(jnp.int32, (nk, heads * lq), 1)
        s = lax.broadcasted_iota(jnp.int32, (nk, heads * lq), 0)
        h = lax.shift_right_logical(j, lq.bit_length() - 1)
        l_idx = j & (lq - 1)
        slope = jnp.exp2(-8.0 * (h + 1).astype(F32) / heads)
        bias[...] = -(slope * LOG2E) * jnp.abs(WINDOW + l_idx - s).astype(F32)

    if prompt:
        @pl.when(c == 0)
        def _():
            krings[...] = jnp.zeros_like(krings)
            vrings[...] = jnp.zeros_like(vrings)
    else:
        kring[0:WINDOW, :] = (kp_ref[...] * ks_ref[...]).astype(BF16)
        vring[0:WINDOW, :] = vp_ref[...].astype(BF16)

    yield
    kw = kw_ref[...]
    for g in range(KV_HEADS):
        lo, hi = g * HEAD_DIM, (g + 1) * HEAD_DIM
        kn_ref[:, lo:hi] = _head_rms(ko_ref[:, lo:hi].astype(F32), kw)
    kring[WINDOW:nk, :] = (kn_ref[...] * ks_ref[...]).astype(BF16)
    vring[WINDOW:nk, :] = vo_ref[...].astype(BF16)

    hw = hb * HEAD_DIM
    pw = hb * lq
    kdup = _dot(kring[...], dup_ref[...]).astype(BF16)
    vt = _dot_nt(eye_ref[...], vring[...]).astype(BF16)
    mask_q = mq_ref[...]
    mask_o = mo_ref[...]
    eye_blk = ep_ref[...]
    ones = jnp.ones((SUBLANES, hw), BF16)
    yield
    if prompt:
        offs = [jnp.where(c >= wc - i, 0.0, jnp.inf) for i in range(wc)]
    pairs = range(heads // hb)
    group = lambda pi: pi // (qpk // hb)
    qbs = []
    for pi in pairs:
        qp = q_ref[:, pi * hw:(pi + 1) * hw]
        qbs.append(jnp.concatenate([qp] * hb, axis=0) * mask_q)
    ssqs = [_dot_nt(ones, qb * qb)[0:1] for qb in qbs]
    raw = [_dot_nt(kdup[:, group(pi) * hw:(group(pi) + 1) * hw], qbs[pi]) for pi in pairs]
    yield
    pns = []
    for pi in pairs:
        rq = lax.rsqrt(ssqs[pi] * (1.0 / HEAD_DIM) + EPS)
        s2 = raw[pi] * rq + bias[:, pi * pw:(pi + 1) * pw]
        sink2 = sink_ref[:, pi * pw:(pi + 1) * pw]
        m_own = jnp.maximum(jnp.max(s2[WINDOW:nk], axis=0, keepdims=True), sink2)
        if prompt:
            mx = m_own
            for i in range(wc):
                band = s2[i * CHUNK:(i + 1) * CHUNK]
                mx = jnp.maximum(mx, jnp.max(band, axis=0, keepdims=True) - offs[i])
            p = jnp.concatenate([jnp.exp2(s2[i * CHUNK:(i + 1) * CHUNK] - (mx + offs[i])) for i in range(wc)]
                                + [jnp.exp2(s2[WINDOW:nk] - mx)], axis=0)
        else:
            mx = jnp.maximum(m_own, jnp.max(s2[0:WINDOW], axis=0, keepdims=True))
            p = jnp.exp2(s2 - mx)
        den = jnp.sum(p, axis=0, keepdims=True) + jnp.exp2(sink2 - mx)
        pns.append((p * (1.0 / den)).astype(BF16))
    ots = [_dot(vt[group(pi) * HEAD_DIM:(group(pi) + 1) * HEAD_DIM, :], pns[pi]).astype(BF16)
           for pi in pairs]
    yield
    obs = [jnp.concatenate([ot] * hb, axis=0) * mask_o for ot in ots]
    outs = [_dot_nt(eye_blk, ob) for ob in obs]
    for pi in pairs:
        o_ref[:, pi * hw:(pi + 1) * hw] = outs[pi].astype(o_ref.dtype)

    if prompt:
        krings.at[1 - cur][0:WINDOW, :] = kring[lq:nk, :]
        vrings.at[1 - cur][0:WINDOW, :] = vring[lq:nk, :]


def _attn_parts(proj, cols, cache_kv, lp, nb, seq, width):
    m = nb * seq
    lq = min(CHUNK, seq)
    nc = seq // lq
    heads = width // HEAD_DIM
    qpk = heads // KV_HEADS
    kvw = KV_HEADS * HEAD_DIM
    prompt = cache_kv is None
    nk = WINDOW + lq
    assert qpk % 2 == 0 and lq & (lq - 1) == 0 and lq % BF16_ROWS == 0 and WINDOW % CHUNK == 0
    assert cols["q"] % width == 0 and cols["k"] % kvw == 0 and cols["v"] % kvw == 0
    assert prompt and lq == CHUNK or not prompt and nc == 1
    row_spec = lambda w, off: pl.BlockSpec((lq, w), lambda b, c: (b * nc + c, off // w))
    full = lambda shape: pl.BlockSpec(shape, lambda b, c: tuple(0 for _ in shape))
    in_specs = [row_spec(width, cols["q"]), row_spec(kvw, cols["k"]), row_spec(kvw, cols["v"])]
    args = [proj, proj, proj]
    if not prompt:
        in_specs += [pl.BlockSpec((WINDOW, kvw), lambda b, c: (b, 0))] * 2
        args += list(cache_kv)
    sink2 = (jnp.repeat(lp["sinks"], lq) * LOG2E).reshape(1, heads * lq)
    hb = MXU_WIDTH // HEAD_DIM if qpk % (MXU_WIDTH // HEAD_DIM) == 0 else 2
    hw = hb * HEAD_DIM
    kv_idx = np.arange(kvw)
    dup_idx = np.arange(KV_HEADS * hw)
    dup = (kv_idx[:, None] // HEAD_DIM == dup_idx[None, :] // hw) & (kv_idx[:, None] % HEAD_DIM == dup_idx[None, :] % HEAD_DIM)
    ql = np.arange(hb * lq)
    dl = np.arange(hw)
    mask_q = ql[:, None] // lq == dl[None, :] // HEAD_DIM
    eye_blk = np.arange(lq)[:, None] == ql[None, :] % lq
    consts = [jnp.asarray(a, BF16) for a in (dup, np.eye(kvw), mask_q, mask_q.T, eye_blk)]
    in_specs += [full((1, HEAD_DIM)), full((1, kvw)), full((1, heads * lq))] + [full(a.shape) for a in consts]
    args += [lp["k_norm_w"], lp["k_scale"], sink2] + consts
    stages = functools.partial(_attn_stages, lq=lq, qpk=qpk, hb=hb, prompt=prompt)
    out_specs = [pl.BlockSpec((lq, width), lambda b, c: (b * nc + c, 0)),
                 pl.BlockSpec((lq, kvw), lambda b, c: (b * nc + c, 0))]
    out_shapes = [jax.ShapeDtypeStruct((m, width), BF16),
                  jax.ShapeDtypeStruct((m, kvw), F32)]
    scratch = [pltpu.VMEM((2, nk, kvw), BF16),
               pltpu.VMEM((2, nk, kvw), BF16),
               pltpu.VMEM((nk, heads * lq), F32)]
    return stages, in_specs, args, out_specs, out_shapes, scratch


def _mixer_kernel(*refs, ssd_stages, attn_stages, n_ssd, n_attn, n_cast):
    (si, so, ss), (ai, ao, as_) = n_ssd, n_attn
    n_in, n_out = si + ai + n_cast, so + ao + n_cast
    ins, outs, scr = refs[:n_in], refs[n_in:n_in + n_out], refs[n_in + n_out:]
    ssd = ssd_stages(ins[:si] + outs[:so] + scr[:ss])
    attn = attn_stages(ins[si:si + ai] + outs[so:so + ao] + scr[ss:])
    _cast_slabs(ins[si + ai:], outs[so + ao:])
    for gen in (attn, ssd, attn, attn, ssd, ssd, attn, ssd, attn, ssd):
        next(gen, None)
    for gen in (ssd, attn):
        assert next(gen, "done") == "done"


def _mixer(proj, dt_raw, conv_left, init_state, cache_kv, lp, nb, seq, width, cols, cast_ws=()):
    s_st, s_in, s_args, s_out, s_shape, s_scr = _ssd_parts(proj, dt_raw, conv_left, init_state, lp, nb, seq, width, cols)
    a_st, a_in, a_args, a_out, a_shape, a_scr = _attn_parts(proj, cols, cache_kv, lp, nb, seq, width)
    nc = seq // min(CHUNK, seq)
    slabs = _cast_slab_specs(cast_ws, nb * nc, lambda b, c: b * nc + c) if cast_ws else []
    if slabs is None:
        slabs, late = [], [w_.astype(BF16) for w_ in cast_ws]
    else:
        late = None
    cast_args = list(cast_ws) if slabs else []
    kern = functools.partial(_mixer_kernel, ssd_stages=s_st, attn_stages=a_st, n_cast=len(slabs),
                             n_ssd=(len(s_in), len(s_out), len(s_scr)), n_attn=(len(a_in), len(a_out), len(a_scr)))
    outs = pl.pallas_call(
        kern,
        grid=(nb, nc),
        in_specs=s_in + a_in + slabs,
        out_specs=s_out + a_out + slabs,
        out_shape=s_shape + a_shape + [jax.ShapeDtypeStruct(w_.shape, BF16) for w_ in cast_args],
        scratch_shapes=s_scr + a_scr,
        compiler_params=_cparams("arbitrary", "arbitrary"),
        name="mixer",
    )(*s_args, *a_args, *cast_args)
    n_main = len(s_out) + len(a_out)
    return outs[:n_main], (late if late is not None else list(outs[n_main:]))


def _prep_layer(l, w_ada, b_ada, g_mix, w_in, conv_w, conv_b, dt_bias, a_log, d_skip, ssd_norm_w,
                q_norm_w, k_norm_w, sinks, w_out, g_ffn, w_gate_up, w_down):
    d = w_in.shape[1]
    width = d // 2
    gn = SSD_GROUPS * SSD_STATE
    kvw = KV_HEADS * HEAD_DIM
    heads = width // HEAD_DIM
    wi = w_in[l]
    o1 = width
    o2 = o1 + width + 2 * gn
    o3 = o2 + heads
    w_all = wi.astype(BF16)
    w_b = w_all[:, o3:]
    w_dt = jnp.pad(w_all[:, o2:o3], ((0, 0), (0, LANES - heads)))
    cols = {"z": 0, "x": width, "q": 2 * width, "B": 3 * width, "C": 3 * width + gn,
            "k": 3 * width + 2 * gn, "v": 3 * width + 2 * gn + kvw}
    tn = 2 * kvw
    assert width % tn == 0 and gn % tn == 0
    src_order = ["z", "x", "B", "C", "q", "k"]
    src_width = {"z": width, "x": width, "B": gn, "C": gn, "q": width, "k": 2 * kvw}
    dest = [cols[name] // tn + t for name in src_order for t in range(src_width[name] // tn)]
    lane_piece = jnp.arange(LANES)
    head_of_lane = jnp.arange(width) // HEAD_DIM
    sel = ((lane_piece[:, None] % PACK_STRIDE == head_of_lane[None, :])
           & (lane_piece[:, None] < 3 * PACK_STRIDE)).astype(BF16)
    pad_h = (0, LANES - heads)
    k_scale = jnp.tile(q_norm_w[l], KV_HEADS) * (HEAD_DIM ** -0.5 * LOG2E)
    return dict(
        w_ada=w_ada[l], b_ada=b_ada[l], g_mix=g_mix[l], g_ffn=g_ffn[l],
        w_a=w_all, a_cols=o2, w_b=w_b, w_dt=w_dt, cols=cols, dest=dest, tn_in=tn,
        w_out=w_out[l], w_gu=w_gate_up[l], w_dn=w_down[l],
        conv_w=conv_w[l], conv_b=conv_b[l].reshape(1, -1),
        dt_bias=jnp.pad(dt_bias[l], pad_h).reshape(1, LANES), a_log=jnp.pad(a_log[l], pad_h).reshape(1, LANES),
        d_skip=jnp.repeat(d_skip[l], HEAD_DIM).reshape(1, width), ssd_norm_w=ssd_norm_w[l].reshape(1, width),
        sel=sel, k_norm_w=k_norm_w[l].reshape(1, HEAD_DIM), k_scale=k_scale.reshape(1, kvw),
        sinks=sinks[l], width=width)


def _layer(x, mods, boff, lp, ssd_init, conv_left, cache_kv, w16=None):
    nb, seq, d = x.shape
    m = nb * seq
    width = lp["width"]
    cols = lp["cols"]
    kvw = KV_HEADS * HEAD_DIM
    sh1, sc1, g1, sh2, sc2, g2 = mods
    x2 = x.reshape(m, d)
    h = _norm_mod(x2, lp["g_mix"], sc1, sh1, seq, boff)
    proj, dt_raw = _mm_in(h, lp["w_a"], lp["a_cols"], lp["w_b"], lp["w_dt"], lp["dest"], lp["tn_in"])
    if cache_kv is not None:
        cache_kv = tuple(t.reshape(nb * WINDOW, kvw) for t in cache_kv)
    make = w16 is None
    (y, ssd_state, conv_state, o, kn), made = _mixer(proj, dt_raw, conv_left, ssd_init, cache_kv, lp, nb, seq, width,
                                                      cols, (lp["w_out"], lp["w_gu"]) if make else ())
    w_out16, w_gu16 = made if make else (w16["w_out"], w16["w_gu"])
    x1 = _mm_out(y, o, w_out16, x2, g1, seq, boff)
    h2 = _norm_mod(x1, lp["g_ffn"], sc2, sh2, seq, boff)
    act, made = _mm_gate_up(h2, w_gu16, (lp["w_dn"],) if make else ())
    w_dn16 = made[0] if make else w16["w_dn"]
    out = _mm_down(act, w_dn16, x1, g2, seq, boff)
    keep = min(WINDOW, seq) if cache_kv is None else seq
    heads = width // HEAD_DIM
    k_state = kn.reshape(nb, seq, KV_HEADS, HEAD_DIM)[:, seq - keep:]
    v_state = proj[:, cols["v"]:cols["v"] + kvw].astype(F32).reshape(nb, seq, KV_HEADS, HEAD_DIM)[:, seq - keep:]
    ssd_state = ssd_state.reshape(nb, heads, HEAD_DIM, SSD_STATE)
    w16 = dict(w_out=w_out16, w_gu=w_gu16, w_dn=w_dn16)
    return out.reshape(nb, seq, d), ssd_state, conv_state, k_state, v_state, w16


def kernel(x_prompt, x_sample, state_ssd, state_conv, cache_k, cache_v, c_prompt, c_sample, w_ada, b_ada, g_mix, w_in, conv_w, conv_b, dt_bias, a_log, d_skip, ssd_norm_w, q_norm_w, k_norm_w, sinks, w_out, g_ffn, w_gate_up, w_down):
    depth = w_in.shape[0]
    bp, _, d = x_prompt.shape
    bs = x_sample.shape[0]
    width = d // 2
    assert cache_k.shape[2] == WINDOW and x_sample.shape[1] <= CHUNK
    yp, ys = x_prompt, x_sample
    outs = [[] for _ in range(8)]
    c_all = jnp.concatenate([c_prompt, c_sample], axis=0)
    zero_conv = jnp.zeros((bp, SSD_CONV - 1, width + 2 * SSD_GROUPS * SSD_STATE), F32)
    for l in range(depth):
        lp = _prep_layer(l, w_ada, b_ada, g_mix, w_in, conv_w, conv_b, dt_bias, a_log, d_skip, ssd_norm_w,
                         q_norm_w, k_norm_w, sinks, w_out, g_ffn, w_gate_up, w_down)
        mod = _ada(c_all, lp["w_ada"], lp["b_ada"]).reshape(bp + bs, N_MOD, 1, d)
        mods = [mod[:, i] for i in range(N_MOD)]
        yp, s1, s2, s3, s4, w16 = _layer(yp, mods, 0, lp, None, zero_conv, None)
        init = state_ssd[l].reshape(bs, width, SSD_STATE)
        ys, t1, t2, t3, t4, _ = _layer(ys, mods, bp, lp, init, state_conv[l], (cache_k[l], cache_v[l]), w16)
        for lst, v in zip(outs, (s1, s2, s3, s4, t1, t2, t3, t4)):
            lst.append(v)
    stacked = [jnp.stack(v, axis=0) for v in outs]
    return (yp, ys, *stacked)
```

```python
import functools
import math

import numpy as np
import jax
import jax.numpy as jnp
from jax import lax
from jax.experimental import pallas as pl
from jax.experimental.pallas import tpu as pltpu

CHUNK = 64
HEAD_DIM = 64
SSD_GROUPS = 4
SSD_STATE = 128
SSD_CONV = 4
KV_HEADS = 4
WINDOW = 128
N_MOD = 6
EPS = 1e-6
LOG2E = math.log2(math.e)

LANES = 128
SUBLANES = 8
BF16_ROWS = 16
MXU_WIDTH = 256
VMEM_LIMIT_BYTES = 56 * 1024 * 1024

SUB_CHUNKS = 2
PACK_STRIDE = 32

F32 = jnp.float32
BF16 = jnp.bfloat16


def _cparams(*sem):
    return pltpu.CompilerParams(dimension_semantics=sem, vmem_limit_bytes=VMEM_LIMIT_BYTES)


def _pick(n, candidates):
    for c in candidates:
        if c <= n and n % c == 0:
            return c
    return n


def _when(possible, cond):
    def deco(f):
        if possible:
            pl.when(cond())(f)
    return deco


def _cast_slab_specs(ws, steps, step_of):
    specs = []
    for w in ws:
        k, n = w.shape
        if k % steps or (k // steps) % BF16_ROWS:
            return None
        specs.append(pl.BlockSpec((k // steps, n), lambda *g: (step_of(*g), 0)))
    return specs


def _cast_slabs(in_refs, out_refs):
    for w_ref, o_ref in zip(in_refs, out_refs):
        o_ref[...] = w_ref[...].astype(o_ref.dtype)


def _silu(x):
    h = 0.5 * x
    return h * jnp.tanh(h) + h


def _dot(a, b):
    return jnp.dot(a, b, preferred_element_type=F32)


def _dot_nt(a, b):
    return lax.dot_general(a, b, (((1,), (1,)), ((), ())), preferred_element_type=F32)


def _dot_tn(a, b):
    return lax.dot_general(a, b, (((0,), (0,)), ((), ())), preferred_element_type=F32)


def _split3(v):
    hi = v.astype(BF16)
    r1 = v - hi.astype(F32)
    mid = r1.astype(BF16)
    lo = (r1 - mid.astype(F32)).astype(BF16)
    return hi, mid, lo


def _dot01_left(m01, v):
    hi, mid, lo = _split3(v)
    return _dot(m01, hi) + (_dot(m01, mid) + _dot(m01, lo))


def _pack3(v):
    hi, mid, lo = _split3(v)
    packed = hi.astype(F32) + pltpu.roll(mid.astype(F32), PACK_STRIDE, 1) + pltpu.roll(lo.astype(F32), 2 * PACK_STRIDE, 1)
    return packed.astype(BF16)


def _ada_kernel(c_ref, w_ref, b_ref, o_ref):
    a = _silu(c_ref[...]).astype(BF16)
    o_ref[...] = _dot(a, w_ref[...].astype(BF16)) + b_ref[...]


def _ada(c, w, b):
    r, d = c.shape
    n = w.shape[1]
    tn = _pick(n, (512, 256, 128))
    return pl.pallas_call(
        _ada_kernel,
        grid=(n // tn,),
        in_specs=[pl.BlockSpec((r, d), lambda j: (0, 0)),
                  pl.BlockSpec((d, tn), lambda j: (0, j)),
                  pl.BlockSpec((1, tn), lambda j: (0, j))],
        out_specs=pl.BlockSpec((r, tn), lambda j: (0, j)),
        out_shape=jax.ShapeDtypeStruct((r, n), F32),
        compiler_params=_cparams("arbitrary"),
        name="ada",
    )(c, w, b.reshape(1, n))


def _norm_kernel(x_ref, g_ref, sc_ref, sh_ref, o_ref):
    x = x_ref[...]
    r = lax.rsqrt(jnp.mean(x * x, axis=-1, keepdims=True) + EPS)
    y = x * r * g_ref[...]
    o_ref[...] = (y * (1.0 + sc_ref[...]) + sh_ref[...]).astype(o_ref.dtype)


def _norm_mod(x, g, sc, sh, seq, boff):
    m, d = x.shape
    tm = _pick(seq, (512, 256, 128, 64, 32, 16, 8))
    per = seq // tm
    mod_spec = pl.BlockSpec((None, 1, d), lambda i: (boff + i // per, 0, 0))
    return pl.pallas_call(
        _norm_kernel,
        grid=(m // tm,),
        in_specs=[pl.BlockSpec((tm, d), lambda i: (i, 0)),
                  pl.BlockSpec((1, d), lambda i: (0, 0)),
                  mod_spec, mod_spec],
        out_specs=pl.BlockSpec((tm, d), lambda i: (i, 0)),
        out_shape=jax.ShapeDtypeStruct((m, d), BF16),
        compiler_params=_cparams("arbitrary"),
        name="norm_mod",
    )(x, g.reshape(1, d), sc, sh)


def _mm_in_kernel(a_ref, wa_ref, wb_ref, wdt_ref, o_ref, dt_ref, *, na):
    a = a_ref[...]
    j = pl.program_id(1)
    w = jnp.where(j < na, wa_ref[...], wb_ref[...])
    o_ref[...] = _dot(a, w).astype(o_ref.dtype)

    @pl.when(j == 0)
    def _():
        dt_ref[...] = _dot(a, wdt_ref[...])


def _mm_in(a, w_a, a_cols, w_b, w_dt, dest, tn):
    m, k = a.shape
    na, nbt = a_cols // tn, w_b.shape[1] // tn
    assert a_cols % tn == 0 and w_b.shape[1] % tn == 0 and sorted(dest) == list(range(na + nbt))
    tm = _pick(m, (1024, 512, 256, 128))

    def out_tile(j):
        r = dest[0]
        for t in range(1, na + nbt):
            r = jnp.where(j >= t, dest[t], r)
        return r

    return pl.pallas_call(
        functools.partial(_mm_in_kernel, na=na),
        grid=(m // tm, na + nbt),
        in_specs=[pl.BlockSpec((tm, k), lambda i, j: (i, 0)),
                  pl.BlockSpec((k, tn), lambda i, j: (0, jnp.minimum(j, na - 1))),
                  pl.BlockSpec((k, tn), lambda i, j: (0, jnp.maximum(j - na, 0))),
                  pl.BlockSpec((k, LANES), lambda i, j: (0, 0))],
        out_specs=[pl.BlockSpec((tm, tn), lambda i, j: (i, out_tile(j))),
                   pl.BlockSpec((tm, LANES), lambda i, j: (i, 0))],
        out_shape=[jax.ShapeDtypeStruct((m, (na + nbt) * tn), BF16),
                   jax.ShapeDtypeStruct((m, LANES), F32)],
        compiler_params=_cparams("arbitrary", "arbitrary"),
        name="mm_in",
    )(a, w_a, w_b, w_dt)


def _gate_operand(gate, tm, tn, seq, boff, nb, ij=lambda i, j: (i, j)):
    if seq % tm == 0:
        per = seq // tm
        return gate, pl.BlockSpec((None, 1, tn), lambda *g: (boff + ij(*g)[0] // per, 0, ij(*g)[1]))
    rows = jnp.repeat(gate[boff:boff + nb, 0], seq, axis=0)
    return rows, pl.BlockSpec((tm, tn), lambda *g: ij(*g))


def _mm_out_kernel(y_ref, o_ref, wa_ref, wb_ref, x_ref, g_ref, out_ref):
    acc = _dot(y_ref[...], wa_ref[...]) + _dot(o_ref[...], wb_ref[...])
    out_ref[...] = x_ref[...] + g_ref[...] * acc


def _mm_out(y, o, w, x, gate, seq, boff):
    m, wd = y.shape
    d = w.shape[1]
    tm = _pick(m, (1024, 512, 256, 128))
    tn = _pick(d, (1024, 512, 256, 128))
    gate, gate_spec = _gate_operand(gate, tm, tn, seq, boff, m // seq)
    return pl.pallas_call(
        _mm_out_kernel,
        grid=(m // tm, d // tn),
        in_specs=[pl.BlockSpec((tm, wd), lambda i, j: (i, 0)),
                  pl.BlockSpec((tm, wd), lambda i, j: (i, 0)),
                  pl.BlockSpec((wd, tn), lambda i, j: (0, j)),
                  pl.BlockSpec((wd, tn), lambda i, j: (1, j)),
                  pl.BlockSpec((tm, tn), lambda i, j: (i, j)),
                  gate_spec],
        out_specs=pl.BlockSpec((tm, tn), lambda i, j: (i, j)),
        out_shape=jax.ShapeDtypeStruct((m, d), F32),
        compiler_params=_cparams("arbitrary", "arbitrary"),
        name="mm_out",
    )(y, o, w, w, x, gate)


def _mm_gu_kernel(h_ref, wg_ref, wu_ref, *rest):
    cast_in, (o_ref, *cast_out) = rest[:len(rest) // 2], rest[len(rest) // 2:]
    h = h_ref[...]
    g = _dot(h, wg_ref[...])
    u = _dot(h, wu_ref[...])
    o_ref[...] = (_silu(g) * u).astype(o_ref.dtype)
    _cast_slabs(cast_in, cast_out)


def _mm_gate_up(h, w, cast_ws=()):
    m, d = h.shape
    ff = w.shape[1] // 2
    tm = _pick(m, (2048, 1024, 512, 256, 128))
    tn = _pick(ff, (512, 256, 128))
    nj = ff // tn
    slabs = _cast_slab_specs(cast_ws, (m // tm) * nj, lambda i, j: i * nj + j) if cast_ws else []
    if slabs is None:
        slabs, late = [], [w_.astype(BF16) for w_ in cast_ws]
    else:
        late = None
    outs = pl.pallas_call(
        _mm_gu_kernel,
        grid=(m // tm, nj),
        in_specs=[pl.BlockSpec((tm, d), lambda i, j: (i, 0)),
                  pl.BlockSpec((d, tn), lambda i, j: (0, j)),
                  pl.BlockSpec((d, tn), lambda i, j: (0, nj + j))] + slabs,
        out_specs=[pl.BlockSpec((tm, tn), lambda i, j: (i, j))] + slabs,
        out_shape=[jax.ShapeDtypeStruct((m, ff), BF16)]
        + ([jax.ShapeDtypeStruct(w_.shape, BF16) for w_ in cast_ws] if slabs else []),
        compiler_params=_cparams("arbitrary", "arbitrary"),
        name="mm_gate_up",
    )(h, w, w, *(cast_ws if slabs else ()))
    return outs[0], (late if late is not None else list(outs[1:]))


def _mm_down_kernel(a_ref, w_ref, x_ref, g_ref, o_ref):
    o_ref[...] = x_ref[...] + g_ref[...] * _dot(a_ref[...], w_ref[...])


def _mm_down(a, w, x, gate, seq, boff):
    m, f = a.shape
    d = w.shape[1]
    tm = _pick(m, (512, 256, 128))
    tn = _pick(d, (512, 256, 128))
    gate, gate_spec = _gate_operand(gate, tm, tn, seq, boff, m // seq, ij=lambda j, i: (i, j))
    return pl.pallas_call(
        _mm_down_kernel,
        grid=(d // tn, m // tm),
        in_specs=[pl.BlockSpec((tm, f), lambda j, i: (i, 0)),
                  pl.BlockSpec((f, tn), lambda j, i: (0, j)),
                  pl.BlockSpec((tm, tn), lambda j, i: (i, j)),
                  gate_spec],
        out_specs=pl.BlockSpec((tm, tn), lambda j, i: (i, j)),
        out_shape=jax.ShapeDtypeStruct((m, d), F32),
        compiler_params=_cparams("arbitrary", "arbitrary"),
        name="mm_down",
    )(a, w, x, gate)


LEFT_ROWS = 3 * BF16_ROWS
SHIFT_ROWS = SSD_CONV * CHUNK + SUBLANES


def _shift_matrix(valid):
    keep = SSD_CONV - 1
    t = np.zeros((SHIFT_ROWS, LEFT_ROWS + valid), np.float32)
    for k in range(SSD_CONV):
        for r in range(valid):
            src = r + k - keep
            if src >= 0:
                t[k * CHUNK + r, LEFT_ROWS + src] = 1.0
            else:
                for piece in range(3):
                    t[k * CHUNK + r, piece * BF16_ROWS + BF16_ROWS + src] = 1.0
    for r in range(keep):
        t[SSD_CONV * CHUNK + r, LEFT_ROWS + valid - keep + r] = 1.0
    return jnp.asarray(t, BF16)


def _ssd_stages(refs, valid, hpg, has_init, c, first, last_chunk):
    if has_init:
        (z_ref, x_ref, b_ref, c_ref, dt_ref, convl_ref, init_ref, tmat_ref, cw_ref, cb_ref, dtb_ref, alog_ref,
         dskip_ref, nw_ref, sel_ref, y_ref, state_ref, convs_ref, left, tmp, st) = refs
    else:
        (z_ref, x_ref, b_ref, c_ref, dt_ref, convl_ref, tmat_ref, cw_ref, cb_ref, dtb_ref, alog_ref,
         dskip_ref, nw_ref, sel_ref, y_ref, state_ref, convs_ref, left, tmp, st) = refs
        init_ref = None
    width = x_ref.shape[1]
    heads = width // HEAD_DIM
    gn = SSD_GROUPS * SSD_STATE
    gw = hpg * HEAD_DIM
    L = CHUNK
    keep = SSD_CONV - 1

    @_when(first, lambda: c == 0)
    def _():
        tmp[...] = jnp.zeros_like(tmp)
        tmp[BF16_ROWS - keep:BF16_ROWS, :] = convl_ref[...]
        hi, mid, lo = _split3(tmp[...])
        left[0:BF16_ROWS, :] = lo
        left[BF16_ROWS:2 * BF16_ROWS, :] = mid
        left[2 * BF16_ROWS:LEFT_ROWS, :] = hi
        if has_init:
            st[...] = init_ref[...].T
        else:
            st[...] = jnp.zeros_like(st)

    tmat = tmat_ref[...]
    bounds = ((0, width), (width, width + gn), (width + gn, width + 2 * gn))
    shifted = [_dot(tmat, jnp.concatenate([left[:, lo:hi], ref[...]], axis=0))
               for ref, (lo, hi) in zip((x_ref, b_ref, c_ref), bounds)]
    yield

    def conv(sh, lo, hi):
        acc = cb_ref[:, lo:hi] + sh[0:L] * cw_ref[0:1, lo:hi]
        for k in range(1, SSD_CONV):
            acc = acc + sh[k * L:(k + 1) * L] * cw_ref[k:k + 1, lo:hi]
        convs_ref[:, lo:hi] = sh[SSD_CONV * L:SSD_CONV * L + keep]
        return _silu(acc)

    xs, bc, cc = [conv(sh, lo, hi) for sh, (lo, hi) in zip(shifted, bounds)]

    left[0:2 * BF16_ROWS, :] = jnp.zeros((2 * BF16_ROWS, left.shape[1]), BF16)
    left[2 * BF16_ROWS:LEFT_ROWS, 0:width] = x_ref[valid - BF16_ROWS:valid, :]
    left[2 * BF16_ROWS:LEFT_ROWS, width:width + gn] = b_ref[valid - BF16_ROWS:valid, :]
    left[2 * BF16_ROWS:LEFT_ROWS, width + gn:width + 2 * gn] = c_ref[valid - BF16_ROWS:valid, :]

    row = lax.broadcasted_iota(jnp.int32, (L, LANES), 0)
    lane = lax.broadcasted_iota(jnp.int32, (L, LANES), 1)
    if valid < L:
        dt_raw = jnp.concatenate([dt_ref[...], jnp.zeros((L - valid, LANES), F32)], axis=0)
    else:
        dt_raw = dt_ref[...]
    u = dt_raw + dtb_ref[...]
    dt = jnp.maximum(u, 0.0) + jnp.log1p(jnp.exp(-jnp.abs(u)))
    dt = jnp.where((row < valid) & (lane < heads), dt, 0.0)
    a = dt * (-jnp.exp(alog_ref[...]))

    tri = jnp.where(lax.broadcasted_iota(jnp.int32, (L, L), 0) >= lax.broadcasted_iota(jnp.int32, (L, L), 1),
                    1.0, 0.0).astype(BF16)
    a_cs = _dot01_left(tri, a)
    both = _dot(jnp.concatenate([_pack3(a_cs), _pack3(dt)], axis=0), sel_ref[...])
    yield
    acs_b = both[0:L]
    dt_b = both[L:2 * L]

    s_idx = lane & (HEAD_DIM - 1)
    causal = jnp.where(row >= s_idx, 0.0, -jnp.inf)
    diag = row == s_idx
    dec = []
    for j in range(width // LANES):
        blk = acs_b[:, j * LANES:(j + 1) * LANES]
        acs_row = jnp.sum(jnp.where(diag, blk, 0.0), axis=0, keepdims=True)
        dec.append(jnp.exp(blk - acs_row + causal))
    decay = jnp.concatenate(dec, axis=1)
    last_b = acs_b[L - 1:L, :]
    e_in = jnp.exp(acs_b)
    e_out = jnp.exp(last_b - acs_b)
    e_last = jnp.exp(last_b)

    xbar = xs * dt_b
    xd16 = (xbar * e_out).astype(BF16)

    pr = lax.broadcasted_iota(jnp.int32, (2 * HEAD_DIM, 2 * HEAD_DIM), 0) >= HEAD_DIM
    pc = lax.broadcasted_iota(jnp.int32, (2 * HEAD_DIM, 2 * HEAD_DIM), 1) >= HEAD_DIM
    pair_mask = pr == pc

    zg = _silu(z_ref[...].astype(F32))
    groups = range(SSD_GROUPS)
    span = lambda g: slice(g * gw, (g + 1) * gw)
    bgs = [bc[:, g * SSD_STATE:(g + 1) * SSD_STATE].astype(BF16) for g in groups]
    cgs = [cc[:, g * SSD_STATE:(g + 1) * SSD_STATE].astype(BF16) for g in groups]
    cb_ts = [_dot_nt(cgs[g], jnp.concatenate([bgs[g]] * hpg, axis=0)) for g in groups]
    st_gs = [st[:, span(g)] for g in groups]
    y_offs = [_dot(cgs[g], st_gs[g].astype(BF16)) for g in groups]
    st_new = [_dot_tn(bgs[g], xd16[:, span(g)]) for g in groups]
    yield
    for g in groups:
        st[:, span(g)] = st_gs[g] * e_last[:, span(g)] + st_new[g]
    y_diags = []
    for g in groups:
        gmat = (cb_ts[g] * decay[:, span(g)]).astype(BF16)
        for p in range(hpg // 2):
            plo = g * gw + p * 2 * HEAD_DIM
            xp = xbar[:, plo:plo + 2 * HEAD_DIM]
            bd = jnp.where(pair_mask, jnp.concatenate([xp, xp], axis=0), 0.0).astype(BF16)
            y_diags.append(_dot(gmat[:, p * 2 * HEAD_DIM:(p + 1) * 2 * HEAD_DIM], bd))
    yield
    for g in groups:
        y_diag = jnp.concatenate(y_diags[g * (hpg // 2):(g + 1) * (hpg // 2)], axis=1)
        y = y_diag + y_offs[g] * e_in[:, span(g)] + dskip_ref[:, span(g)] * xs[:, span(g)]
        gt = y[0:valid, :] * zg[:, span(g)]
        rs = lax.rsqrt(jnp.mean(gt * gt, axis=-1, keepdims=True) + EPS)
        y_ref[:, span(g)] = (gt * rs * nw_ref[:, span(g)]).astype(y_ref.dtype)

    @_when(last_chunk is not None, lambda: c == last_chunk)
    def _():
        state_ref[...] = st[...].T


def _ssd_parts(proj, dt_raw, conv_left, init_state, lp, nb, seq, width, cols, subs):
    m = proj.shape[0]
    valid = min(CHUNK, seq)
    nc = seq // valid
    heads = width // HEAD_DIM
    hpg = heads // SSD_GROUPS
    gn = SSD_GROUPS * SSD_STATE
    cch = width + 2 * gn
    has_init = init_state is not None
    assert valid % BF16_ROWS == 0 and hpg % 2 == 0 and heads <= PACK_STRIDE and 3 * PACK_STRIDE <= LANES

    def rows(b, c):
        return b * (nc // subs) + c

    def col_spec(w, off):
        assert off % w == 0
        return pl.BlockSpec((subs * valid, w), lambda b, c: (rows(b, c), off // w))

    full = lambda shape: pl.BlockSpec(shape, lambda b, c: tuple(0 for _ in shape))
    in_specs = [col_spec(width, cols["z"]), col_spec(width, cols["x"]),
                col_spec(gn, cols["B"]), col_spec(gn, cols["C"]),
                pl.BlockSpec((subs * valid, LANES), lambda b, c: (rows(b, c), 0)),
                pl.BlockSpec((None, SSD_CONV - 1, cch), lambda b, c: (b, 0, 0))]
    args = [proj, proj, proj, proj, dt_raw, conv_left]
    if has_init:
        in_specs.append(pl.BlockSpec((None, width, SSD_STATE), lambda b, c: (b, 0, 0)))
        args.append(init_state)
    tmat = _shift_matrix(valid)
    in_specs += [full(tmat.shape), full((SSD_CONV, cch)), full((1, cch)), full((1, LANES)), full((1, LANES)),
                 full((1, width)), full((1, width)), full((LANES, width))]
    args += [tmat, lp["conv_w"], lp["conv_b"], lp["dt_bias"], lp["a_log"], lp["d_skip"], lp["ssd_norm_w"], lp["sel"]]
    stages = functools.partial(_ssd_stages, valid=valid, hpg=hpg, has_init=has_init)
    out_specs = [pl.BlockSpec((subs * valid, width), lambda b, c: (rows(b, c), 0)),
                 pl.BlockSpec((None, width, SSD_STATE), lambda b, c: (b, 0, 0)),
                 pl.BlockSpec((None, SSD_CONV - 1, cch), lambda b, c: (b, 0, 0))]
    out_shapes = [jax.ShapeDtypeStruct((m, width), BF16),
                  jax.ShapeDtypeStruct((nb, width, SSD_STATE), F32),
                  jax.ShapeDtypeStruct((nb, SSD_CONV - 1, cch), F32)]
    scratch = [pltpu.VMEM((LEFT_ROWS, cch), BF16),
               pltpu.VMEM((BF16_ROWS, cch), F32),
               pltpu.VMEM((SSD_STATE, width), F32)]
    return stages, in_specs, args, out_specs, out_shapes, scratch


def _head_rms(x, w):
    return x * lax.rsqrt(jnp.mean(x * x, axis=-1, keepdims=True) + EPS) * w


def _attn_stages(refs, lq, qpk, hb, prompt, b, c, first, cur):
    if prompt:
        q_ref, ko_ref, vo_ref = refs[:3]
        rest = refs[3:]
    else:
        q_ref, ko_ref, vo_ref, kp_ref, vp_ref = refs[:5]
        rest = refs[5:]
    kw_ref, ks_ref, sink_ref, dup_ref, eye_ref, mq_ref, mo_ref, ep_ref, o_ref, kn_ref, krings, vrings, bias = rest
    nk = WINDOW + lq
    heads = KV_HEADS * qpk
    wc = WINDOW // CHUNK
    kring = krings.at[cur]
    vring = vrings.at[cur]

    @_when(first, lambda: (b == 0) & (c == 0))
    def _():
        j = lax.broadcasted_iota(jnp.int32, (nk, heads * lq), 1)
        s = lax.broadcasted_iota(jnp.int32, (nk, heads * lq), 0)
        h = lax.shift_right_logical(j, lq.bit_length() - 1)
        l_idx = j & (lq - 1)
        slope = jnp.exp2(-8.0 * (h + 1).astype(F32) / heads)
        bias[...] = -(slope * LOG2E) * jnp.abs(WINDOW + l_idx - s).astype(F32)

    if prompt:
        @_when(first, lambda: c == 0)
        def _():
            krings[...] = jnp.zeros_like(krings)
            vrings[...] = jnp.zeros_like(vrings)
    else:
        kring[0:WINDOW, :] = (kp_ref[...] * ks_ref[...]).astype(BF16)
        vring[0:WINDOW, :] = vp_ref[...].astype(BF16)

    yield
    kw = kw_ref[...]
    for g in range(KV_HEADS):
        lo, hi = g * HEAD_DIM, (g + 1) * HEAD_DIM
        kn_ref[:, lo:hi] = _head_rms(ko_ref[:, lo:hi].astype(F32), kw)
    kring[WINDOW:nk, :] = (kn_ref[...] * ks_ref[...]).astype(BF16)
    vring[WINDOW:nk, :] = vo_ref[...].astype(BF16)

    hw = hb * HEAD_DIM
    pw = hb * lq
    kdup = _dot(kring[...], dup_ref[...]).astype(BF16)
    vt = _dot_nt(eye_ref[...], vring[...]).astype(BF16)
    mask_q = mq_ref[...]
    mask_o = mo_ref[...]
    eye_blk = ep_ref[...]
    ones = jnp.ones((SUBLANES, hw), BF16)
    yield
    if prompt:
        offs = [jnp.where(c >= wc - i, 0.0, jnp.inf) for i in range(wc)]
    pairs = range(heads // hb)
    group = lambda pi: pi // (qpk // hb)
    qbs = []
    for pi in pairs:
        qp = q_ref[:, pi * hw:(pi + 1) * hw]
        qbs.append(jnp.concatenate([qp] * hb, axis=0) * mask_q)
    ssqs = [_dot_nt(ones, qb * qb)[0:1] for qb in qbs]
    raw = [_dot_nt(kdup[:, group(pi) * hw:(group(pi) + 1) * hw], qbs[pi]) for pi in pairs]
    yield
    pns = []
    for pi in pairs:
        rq = lax.rsqrt(ssqs[pi] * (1.0 / HEAD_DIM) + EPS)
        s2 = raw[pi] * rq + bias[:, pi * pw:(pi + 1) * pw]
        sink2 = sink_ref[:, pi * pw:(pi + 1) * pw]
        m_own = jnp.maximum(jnp.max(s2[WINDOW:nk], axis=0, keepdims=True), sink2)
        if prompt:
            mx = m_own
            for i in range(wc):
                band = s2[i * CHUNK:(i + 1) * CHUNK]
                mx = jnp.maximum(mx, jnp.max(band, axis=0, keepdims=True) - offs[i])
            p = jnp.concatenate([jnp.exp2(s2[i * CHUNK:(i + 1) * CHUNK] - (mx + offs[i])) for i in range(wc)]
                                + [jnp.exp2(s2[WINDOW:nk] - mx)], axis=0)
        else:
            mx = jnp.maximum(m_own, jnp.max(s2[0:WINDOW], axis=0, keepdims=True))
            p = jnp.exp2(s2 - mx)
        den = jnp.sum(p, axis=0, keepdims=True) + jnp.exp2(sink2 - mx)
        pns.append((p * (1.0 / den)).astype(BF16))
    ots = [_dot(vt[group(pi) * HEAD_DIM:(group(pi) + 1) * HEAD_DIM, :], pns[pi]).astype(BF16)
           for pi in pairs]
    yield
    obs = [jnp.concatenate([ot] * hb, axis=0) * mask_o for ot in ots]
    outs = [_dot_nt(eye_blk, ob) for ob in obs]
    for pi in pairs:
        o_ref[:, pi * hw:(pi + 1) * hw] = outs[pi].astype(o_ref.dtype)

    if prompt:
        krings.at[1 - cur][0:WINDOW, :] = kring[lq:nk, :]
        vrings.at[1 - cur][0:WINDOW, :] = vring[lq:nk, :]


def _attn_parts(proj, cols, cache_kv, lp, nb, seq, width, subs):
    m = nb * seq
    lq = min(CHUNK, seq)
    nc = seq // lq
    heads = width // HEAD_DIM
    qpk = heads // KV_HEADS
    kvw = KV_HEADS * HEAD_DIM
    prompt = cache_kv is None
    nk = WINDOW + lq
    assert qpk % 2 == 0 and lq & (lq - 1) == 0 and lq % BF16_ROWS == 0 and WINDOW % CHUNK == 0
    assert cols["q"] % width == 0 and cols["k"] % kvw == 0 and cols["v"] % kvw == 0
    assert prompt and lq == CHUNK or not prompt and nc == 1
    row_spec = lambda w, off: pl.BlockSpec((subs * lq, w), lambda b, c: (b * (nc // subs) + c, off // w))
    full = lambda shape: pl.BlockSpec(shape, lambda b, c: tuple(0 for _ in shape))
    in_specs = [row_spec(width, cols["q"]), row_spec(kvw, cols["k"]), row_spec(kvw, cols["v"])]
    args = [proj, proj, proj]
    if not prompt:
        in_specs += [pl.BlockSpec((WINDOW, kvw), lambda b, c: (b, 0))] * 2
        args += list(cache_kv)
    sink2 = (jnp.repeat(lp["sinks"], lq) * LOG2E).reshape(1, heads * lq)
    hb = MXU_WIDTH // HEAD_DIM if qpk % (MXU_WIDTH // HEAD_DIM) == 0 else 2
    hw = hb * HEAD_DIM
    kv_idx = np.arange(kvw)
    dup_idx = np.arange(KV_HEADS * hw)
    dup = (kv_idx[:, None] // HEAD_DIM == dup_idx[None, :] // hw) & (kv_idx[:, None] % HEAD_DIM == dup_idx[None, :] % HEAD_DIM)
    ql = np.arange(hb * lq)
    dl = np.arange(hw)
    mask_q = ql[:, None] // lq == dl[None, :] // HEAD_DIM
    eye_blk = np.arange(lq)[:, None] == ql[None, :] % lq
    consts = [jnp.asarray(a, BF16) for a in (dup, np.eye(kvw), mask_q, mask_q.T, eye_blk)]
    in_specs += [full((1, HEAD_DIM)), full((1, kvw)), full((1, heads * lq))] + [full(a.shape) for a in consts]
    args += [lp["k_norm_w"], lp["k_scale"], sink2] + consts
    stages = functools.partial(_attn_stages, lq=lq, qpk=qpk, hb=hb, prompt=prompt)
    out_specs = [pl.BlockSpec((subs * lq, width), lambda b, c: (b * (nc // subs) + c, 0)),
                 pl.BlockSpec((subs * lq, kvw), lambda b, c: (b * (nc // subs) + c, 0))]
    out_shapes = [jax.ShapeDtypeStruct((m, width), BF16),
                  jax.ShapeDtypeStruct((m, kvw), F32)]
    scratch = [pltpu.VMEM((2, nk, kvw), BF16),
               pltpu.VMEM((2, nk, kvw), BF16),
               pltpu.VMEM((nk, heads * lq), F32)]
    return stages, in_specs, args, out_specs, out_shapes, scratch


SSD_ROW_REFS = ((0, 1, 2, 3, 4), (0,))
ATTN_ROW_REFS = ((0, 1, 2), (0, 1))


def _mixer_kernel(*refs, ssd_stages, attn_stages, n_ssd, n_attn, n_cast, subs, nc, rows):
    (si, so, ss), (ai, ao, as_) = n_ssd, n_attn
    n_in, n_out = si + ai + n_cast, so + ao + n_cast
    ins, outs, scr = refs[:n_in], refs[n_in:n_in + n_out], refs[n_in + n_out:]
    b = pl.program_id(0)

    def chunk_rows(group, which, s):
        return tuple(r.at[s * rows:(s + 1) * rows] if i in which else r for i, r in enumerate(group))

    for s in range(subs):
        c = pl.program_id(1) * subs + s
        ssd = ssd_stages(chunk_rows(ins[:si], SSD_ROW_REFS[0], s) + chunk_rows(outs[:so], SSD_ROW_REFS[1], s) + scr[:ss],
                         c=c, first=s == 0, last_chunk=nc - 1 if s == subs - 1 else None)
        attn = attn_stages(chunk_rows(ins[si:si + ai], ATTN_ROW_REFS[0], s)
                           + chunk_rows(outs[so:so + ao], ATTN_ROW_REFS[1], s) + scr[ss:],
                           b=b, c=c, first=s == 0, cur=s % 2 if subs % 2 == 0 else lax.rem(c, 2))
        for gen in (attn, ssd, attn, attn, ssd, ssd, attn, ssd, attn, ssd):
            next(gen, None)
        for gen in (ssd, attn):
            assert next(gen, "done") == "done"
    _cast_slabs(ins[si + ai:], outs[so + ao:])


def _mixer(proj, dt_raw, conv_left, init_state, cache_kv, lp, nb, seq, width, cols, cast_ws=()):
    rows = min(CHUNK, seq)
    nc = seq // rows
    subs = SUB_CHUNKS if nc % SUB_CHUNKS == 0 else 1
    steps = nc // subs
    s_st, s_in, s_args, s_out, s_shape, s_scr = _ssd_parts(proj, dt_raw, conv_left, init_state, lp, nb, seq, width, cols, subs)
    a_st, a_in, a_args, a_out, a_shape, a_scr = _attn_parts(proj, cols, cache_kv, lp, nb, seq, width, subs)
    slabs = _cast_slab_specs(cast_ws, nb * steps, lambda b, c: b * steps + c) if cast_ws else []
    if slabs is None:
        slabs, late = [], [w_.astype(BF16) for w_ in cast_ws]
    else:
        late = None
    cast_args = list(cast_ws) if slabs else []
    kern = functools.partial(_mixer_kernel, ssd_stages=s_st, attn_stages=a_st, n_cast=len(slabs), subs=subs, nc=nc, rows=rows,
                             n_ssd=(len(s_in), len(s_out), len(s_scr)), n_attn=(len(a_in), len(a_out), len(a_scr)))
    outs = pl.pallas_call(
        kern,
        grid=(nb, steps),
        in_specs=s_in + a_in + slabs,
        out_specs=s_out + a_out + slabs,
        out_shape=s_shape + a_shape + [jax.ShapeDtypeStruct(w_.shape, BF16) for w_ in cast_args],
        scratch_shapes=s_scr + a_scr,
        compiler_params=_cparams("arbitrary", "arbitrary"),
        name="mixer",
    )(*s_args, *a_args, *cast_args)
    n_main = len(s_out) + len(a_out)
    return outs[:n_main], (late if late is not None else list(outs[n_main:]))


def _prep_layer(l, w_ada, b_ada, g_mix, w_in, conv_w, conv_b, dt_bias, a_log, d_skip, ssd_norm_w,
                q_norm_w, k_norm_w, sinks, w_out, g_ffn, w_gate_up, w_down):
    d = w_in.shape[1]
    width = d // 2
    gn = SSD_GROUPS * SSD_STATE
    kvw = KV_HEADS * HEAD_DIM
    heads = width // HEAD_DIM
    wi = w_in[l]
    o1 = width
    o2 = o1 + width + 2 * gn
    o3 = o2 + heads
    w_all = wi.astype(BF16)
    w_b = w_all[:, o3:]
    w_dt = jnp.pad(w_all[:, o2:o3], ((0, 0), (0, LANES - heads)))
    cols = {"z": 0, "x": width, "q": 2 * width, "B": 3 * width, "C": 3 * width + gn,
            "k": 3 * width + 2 * gn, "v": 3 * width + 2 * gn + kvw}
    tn = 2 * kvw
    assert width % tn == 0 and gn % tn == 0
    src_order = ["z", "x", "B", "C", "q", "k"]
    src_width = {"z": width, "x": width, "B": gn, "C": gn, "q": width, "k": 2 * kvw}
    dest = [cols[name] // tn + t for name in src_order for t in range(src_width[name] // tn)]
    lane_piece = jnp.arange(LANES)
    head_of_lane = jnp.arange(width) // HEAD_DIM
    sel = ((lane_piece[:, None] % PACK_STRIDE == head_of_lane[None, :])
           & (lane_piece[:, None] < 3 * PACK_STRIDE)).astype(BF16)
    pad_h = (0, LANES - heads)
    k_scale = jnp.tile(q_norm_w[l], KV_HEADS) * (HEAD_DIM ** -0.5 * LOG2E)
    return dict(
        w_ada=w_ada[l], b_ada=b_ada[l], g_mix=g_mix[l], g_ffn=g_ffn[l],
        w_a=w_all, a_cols=o2, w_b=w_b, w_dt=w_dt, cols=cols, dest=dest, tn_in=tn,
        w_out=w_out[l], w_gu=w_gate_up[l], w_dn=w_down[l],
        conv_w=conv_w[l], conv_b=conv_b[l].reshape(1, -1),
        dt_bias=jnp.pad(dt_bias[l], pad_h).reshape(1, LANES), a_log=jnp.pad(a_log[l], pad_h).reshape(1, LANES),
        d_skip=jnp.repeat(d_skip[l], HEAD_DIM).reshape(1, width), ssd_norm_w=ssd_norm_w[l].reshape(1, width),
        sel=sel, k_norm_w=k_norm_w[l].reshape(1, HEAD_DIM), k_scale=k_scale.reshape(1, kvw),
        sinks=sinks[l], width=width)


def _layer(x, mods, boff, lp, ssd_init, conv_left, cache_kv, w16=None):
    nb, seq, d = x.shape
    m = nb * seq
    width = lp["width"]
    cols = lp["cols"]
    kvw = KV_HEADS * HEAD_DIM
    sh1, sc1, g1, sh2, sc2, g2 = mods
    x2 = x.reshape(m, d)
    h = _norm_mod(x2, lp["g_mix"], sc1, sh1, seq, boff)
    proj, dt_raw = _mm_in(h, lp["w_a"], lp["a_cols"], lp["w_b"], lp["w_dt"], lp["dest"], lp["tn_in"])
    if cache_kv is not None:
        cache_kv = tuple(t.reshape(nb * WINDOW, kvw) for t in cache_kv)
    make = w16 is None
    (y, ssd_state, conv_state, o, kn), made = _mixer(proj, dt_raw, conv_left, ssd_init, cache_kv, lp, nb, seq, width,
                                                      cols, (lp["w_out"], lp["w_gu"]) if make else ())
    w_out16, w_gu16 = made if make else (w16["w_out"], w16["w_gu"])
    x1 = _mm_out(y, o, w_out16, x2, g1, seq, boff)
    h2 = _norm_mod(x1, lp["g_ffn"], sc2, sh2, seq, boff)
    act, made = _mm_gate_up(h2, w_gu16, (lp["w_dn"],) if make else ())
    w_dn16 = made[0] if make else w16["w_dn"]
    out = _mm_down(act, w_dn16, x1, g2, seq, boff)
    keep = min(WINDOW, seq) if cache_kv is None else seq
    heads = width // HEAD_DIM
    k_state = kn.reshape(nb, seq, KV_HEADS, HEAD_DIM)[:, seq - keep:]
    v_state = proj[:, cols["v"]:cols["v"] + kvw].astype(F32).reshape(nb, seq, KV_HEADS, HEAD_DIM)[:, seq - keep:]
    ssd_state = ssd_state.reshape(nb, heads, HEAD_DIM, SSD_STATE)
    w16 = dict(w_out=w_out16, w_gu=w_gu16, w_dn=w_dn16)
    return out.reshape(nb, seq, d), ssd_state, conv_state, k_state, v_state, w16


def kernel(x_prompt, x_sample, state_ssd, state_conv, cache_k, cache_v, c_prompt, c_sample, w_ada, b_ada, g_mix, w_in, conv_w, conv_b, dt_bias, a_log, d_skip, ssd_norm_w, q_norm_w, k_norm_w, sinks, w_out, g_ffn, w_gate_up, w_down):
    depth = w_in.shape[0]
    bp, _, d = x_prompt.shape
    bs = x_sample.shape[0]
    width = d // 2
    assert cache_k.shape[2] == WINDOW and x_sample.shape[1] <= CHUNK
    yp, ys = x_prompt, x_sample
    outs = [[] for _ in range(8)]
    c_all = jnp.concatenate([c_prompt, c_sample], axis=0)
    zero_conv = jnp.zeros((bp, SSD_CONV - 1, width + 2 * SSD_GROUPS * SSD_STATE), F32)
    for l in range(depth):
        lp = _prep_layer(l, w_ada, b_ada, g_mix, w_in, conv_w, conv_b, dt_bias, a_log, d_skip, ssd_norm_w,
                         q_norm_w, k_norm_w, sinks, w_out, g_ffn, w_gate_up, w_down)
        mod = _ada(c_all, lp["w_ada"], lp["b_ada"]).reshape(bp + bs, N_MOD, 1, d)
        mods = [mod[:, i] for i in range(N_MOD)]
        yp, s1, s2, s3, s4, w16 = _layer(yp, mods, 0, lp, None, zero_conv, None)
        init = state_ssd[l].reshape(bs, width, SSD_STATE)
        ys, t1, t2, t3, t4, _ = _layer(ys, mods, bp, lp, init, state_conv[l], (cache_k[l], cache_v[l]), w16)
        for lst, v in zip(outs, (s1, s2, s3, s4, t1, t2, t3, t4)):
            lst.append(v)
    stacked = [jnp.stack(v, axis=0) for v in outs]
    return (yp, ys, *stacked)
```

```python
import functools
import math

import numpy as np
import jax
import jax.numpy as jnp
from jax import lax
from jax.experimental import pallas as pl
from jax.experimental.pallas import tpu as pltpu

CHUNK = 64
HEAD_DIM = 64
SSD_GROUPS = 4
SSD_STATE = 128
SSD_CONV = 4
KV_HEADS = 4
WINDOW = 128
N_MOD = 6
EPS = 1e-6
LOG2E = math.log2(math.e)

LANES = 128
SUBLANES = 8
BF16_ROWS = 16
MXU_WIDTH = 256
VMEM_LIMIT_BYTES = 56 * 1024 * 1024

SUB_CHUNKS = 2
WINDOW_BUFFERS = SUB_CHUNKS + 1
PACK_STRIDE = 32

F32 = jnp.float32
BF16 = jnp.bfloat16


def _cparams(*sem):
    return pltpu.CompilerParams(dimension_semantics=sem, vmem_limit_bytes=VMEM_LIMIT_BYTES)


def _pick(n, candidates):
    for c in candidates:
        if c <= n and n % c == 0:
            return c
    return n


def _when(possible, cond):
    def deco(f):
        if possible:
            pl.when(cond())(f)
    return deco


def _cast_slab_specs(ws, steps, step_of):
    specs = []
    for w in ws:
        k, n = w.shape
        if k % steps or (k // steps) % BF16_ROWS:
            return None
        specs.append(pl.BlockSpec((k // steps, n), lambda *g: (step_of(*g), 0)))
    return specs


def _cast_slabs(in_refs, out_refs):
    for w_ref, o_ref in zip(in_refs, out_refs):
        o_ref[...] = w_ref[...].astype(o_ref.dtype)


def _silu(x):
    h = 0.5 * x
    return h * jnp.tanh(h) + h


def _dot(a, b):
    return jnp.dot(a, b, preferred_element_type=F32)


def _dot_nt(a, b):
    return lax.dot_general(a, b, (((1,), (1,)), ((), ())), preferred_element_type=F32)


def _dot_tn(a, b):
    return lax.dot_general(a, b, (((0,), (0,)), ((), ())), preferred_element_type=F32)


def _split3(v):
    hi = v.astype(BF16)
    r1 = v - hi.astype(F32)
    mid = r1.astype(BF16)
    lo = (r1 - mid.astype(F32)).astype(BF16)
    return hi, mid, lo


def _dot01_left(m01, v):
    hi, mid, lo = _split3(v)
    return _dot(m01, hi) + (_dot(m01, mid) + _dot(m01, lo))


def _pack3(v):
    hi, mid, lo = _split3(v)
    packed = hi.astype(F32) + pltpu.roll(mid.astype(F32), PACK_STRIDE, 1) + pltpu.roll(lo.astype(F32), 2 * PACK_STRIDE, 1)
    return packed.astype(BF16)


def _ada_kernel(c_ref, w_ref, b_ref, o_ref):
    a = _silu(c_ref[...]).astype(BF16)
    o_ref[...] = _dot(a, w_ref[...].astype(BF16)) + b_ref[...]


def _ada(c, w, b):
    r, d = c.shape
    n = w.shape[1]
    tn = _pick(n, (512, 256, 128))
    return pl.pallas_call(
        _ada_kernel,
        grid=(n // tn,),
        in_specs=[pl.BlockSpec((r, d), lambda j: (0, 0)),
                  pl.BlockSpec((d, tn), lambda j: (0, j)),
                  pl.BlockSpec((1, tn), lambda j: (0, j))],
        out_specs=pl.BlockSpec((r, tn), lambda j: (0, j)),
        out_shape=jax.ShapeDtypeStruct((r, n), F32),
        compiler_params=_cparams("arbitrary"),
        name="ada",
    )(c, w, b.reshape(1, n))


def _norm_kernel(x_ref, g_ref, sc_ref, sh_ref, o_ref):
    x = x_ref[...]
    r = lax.rsqrt(jnp.mean(x * x, axis=-1, keepdims=True) + EPS)
    y = x * r * g_ref[...]
    o_ref[...] = (y * (1.0 + sc_ref[...]) + sh_ref[...]).astype(o_ref.dtype)


def _norm_mod(x, g, sc, sh, seq, boff):
    m, d = x.shape
    tm = _pick(seq, (512, 256, 128, 64, 32, 16, 8))
    per = seq // tm
    mod_spec = pl.BlockSpec((None, 1, d), lambda i: (boff + i // per, 0, 0))
    return pl.pallas_call(
        _norm_kernel,
        grid=(m // tm,),
        in_specs=[pl.BlockSpec((tm, d), lambda i: (i, 0)),
                  pl.BlockSpec((1, d), lambda i: (0, 0)),
                  mod_spec, mod_spec],
        out_specs=pl.BlockSpec((tm, d), lambda i: (i, 0)),
        out_shape=jax.ShapeDtypeStruct((m, d), BF16),
        compiler_params=_cparams("arbitrary"),
        name="norm_mod",
    )(x, g.reshape(1, d), sc, sh)


def _mm_in_kernel(a_ref, wa_ref, wb_ref, wdt_ref, o_ref, dt_ref, *, na):
    a = a_ref[...]
    j = pl.program_id(1)
    w = jnp.where(j < na, wa_ref[...], wb_ref[...])
    o_ref[...] = _dot(a, w).astype(o_ref.dtype)

    @pl.when(j == 0)
    def _():
        dt_ref[...] = _dot(a, wdt_ref[...])


def _mm_in(a, w_a, a_cols, w_b, w_dt, dest, tn):
    m, k = a.shape
    na, nbt = a_cols // tn, w_b.shape[1] // tn
    assert a_cols % tn == 0 and w_b.shape[1] % tn == 0 and sorted(dest) == list(range(na + nbt))
    tm = _pick(m, (1024, 512, 256, 128))

    def out_tile(j):
        r = dest[0]
        for t in range(1, na + nbt):
            r = jnp.where(j >= t, dest[t], r)
        return r

    return pl.pallas_call(
        functools.partial(_mm_in_kernel, na=na),
        grid=(m // tm, na + nbt),
        in_specs=[pl.BlockSpec((tm, k), lambda i, j: (i, 0)),
                  pl.BlockSpec((k, tn), lambda i, j: (0, jnp.minimum(j, na - 1))),
                  pl.BlockSpec((k, tn), lambda i, j: (0, jnp.maximum(j - na, 0))),
                  pl.BlockSpec((k, LANES), lambda i, j: (0, 0))],
        out_specs=[pl.BlockSpec((tm, tn), lambda i, j: (i, out_tile(j))),
                   pl.BlockSpec((tm, LANES), lambda i, j: (i, 0))],
        out_shape=[jax.ShapeDtypeStruct((m, (na + nbt) * tn), BF16),
                   jax.ShapeDtypeStruct((m, LANES), F32)],
        compiler_params=_cparams("arbitrary", "arbitrary"),
        name="mm_in",
    )(a, w_a, w_b, w_dt)


def _gate_operand(gate, tm, tn, seq, boff, nb, ij=lambda i, j: (i, j)):
    if seq % tm == 0:
        per = seq // tm
        return gate, pl.BlockSpec((None, 1, tn), lambda *g: (boff + ij(*g)[0] // per, 0, ij(*g)[1]))
    rows = jnp.repeat(gate[boff:boff + nb, 0], seq, axis=0)
    return rows, pl.BlockSpec((tm, tn), lambda *g: ij(*g))


def _mm_out_kernel(y_ref, o_ref, wa_ref, wb_ref, x_ref, g_ref, out_ref):
    acc = _dot(y_ref[...], wa_ref[...]) + _dot(o_ref[...], wb_ref[...])
    out_ref[...] = x_ref[...] + g_ref[...] * acc


def _mm_out(y, o, w, x, gate, seq, boff):
    m, wd = y.shape
    d = w.shape[1]
    tm = _pick(m, (1024, 512, 256, 128))
    tn = _pick(d, (1024, 512, 256, 128))
    gate, gate_spec = _gate_operand(gate, tm, tn, seq, boff, m // seq)
    return pl.pallas_call(
        _mm_out_kernel,
        grid=(m // tm, d // tn),
        in_specs=[pl.BlockSpec((tm, wd), lambda i, j: (i, 0)),
                  pl.BlockSpec((tm, wd), lambda i, j: (i, 0)),
                  pl.BlockSpec((wd, tn), lambda i, j: (0, j)),
                  pl.BlockSpec((wd, tn), lambda i, j: (1, j)),
                  pl.BlockSpec((tm, tn), lambda i, j: (i, j)),
                  gate_spec],
        out_specs=pl.BlockSpec((tm, tn), lambda i, j: (i, j)),
        out_shape=jax.ShapeDtypeStruct((m, d), F32),
        compiler_params=_cparams("arbitrary", "arbitrary"),
        name="mm_out",
    )(y, o, w, w, x, gate)


def _mm_gu_kernel(h_ref, wg_ref, wu_ref, *rest):
    cast_in, (o_ref, *cast_out) = rest[:len(rest) // 2], rest[len(rest) // 2:]
    h = h_ref[...]
    g = _dot(h, wg_ref[...])
    u = _dot(h, wu_ref[...])
    o_ref[...] = (_silu(g) * u).astype(o_ref.dtype)
    _cast_slabs(cast_in, cast_out)


def _mm_gate_up(h, w, cast_ws=()):
    m, d = h.shape
    ff = w.shape[1] // 2
    tm = _pick(m, (2048, 1024, 512, 256, 128))
    tn = _pick(ff, (512, 256, 128))
    nj = ff // tn
    slabs = _cast_slab_specs(cast_ws, (m // tm) * nj, lambda i, j: i * nj + j) if cast_ws else []
    if slabs is None:
        slabs, late = [], [w_.astype(BF16) for w_ in cast_ws]
    else:
        late = None
    outs = pl.pallas_call(
        _mm_gu_kernel,
        grid=(m // tm, nj),
        in_specs=[pl.BlockSpec((tm, d), lambda i, j: (i, 0)),
                  pl.BlockSpec((d, tn), lambda i, j: (0, j)),
                  pl.BlockSpec((d, tn), lambda i, j: (0, nj + j))] + slabs,
        out_specs=[pl.BlockSpec((tm, tn), lambda i, j: (i, j))] + slabs,
        out_shape=[jax.ShapeDtypeStruct((m, ff), BF16)]
        + ([jax.ShapeDtypeStruct(w_.shape, BF16) for w_ in cast_ws] if slabs else []),
        compiler_params=_cparams("arbitrary", "arbitrary"),
        name="mm_gate_up",
    )(h, w, w, *(cast_ws if slabs else ()))
    return outs[0], (late if late is not None else list(outs[1:]))


def _mm_down_kernel(a_ref, w_ref, x_ref, g_ref, o_ref):
    o_ref[...] = x_ref[...] + g_ref[...] * _dot(a_ref[...], w_ref[...])


def _mm_down(a, w, x, gate, seq, boff):
    m, f = a.shape
    d = w.shape[1]
    tm = _pick(m, (512, 256, 128))
    tn = _pick(d, (512, 256, 128))
    gate, gate_spec = _gate_operand(gate, tm, tn, seq, boff, m // seq, ij=lambda j, i: (i, j))
    return pl.pallas_call(
        _mm_down_kernel,
        grid=(d // tn, m // tm),
        in_specs=[pl.BlockSpec((tm, f), lambda j, i: (i, 0)),
                  pl.BlockSpec((f, tn), lambda j, i: (0, j)),
                  pl.BlockSpec((tm, tn), lambda j, i: (i, j)),
                  gate_spec],
        out_specs=pl.BlockSpec((tm, tn), lambda j, i: (i, j)),
        out_shape=jax.ShapeDtypeStruct((m, d), F32),
        compiler_params=_cparams("arbitrary", "arbitrary"),
        name="mm_down",
    )(a, w, x, gate)


LEFT_ROWS = 3 * BF16_ROWS
SHIFT_ROWS = SSD_CONV * CHUNK + SUBLANES


def _shift_matrix(valid):
    keep = SSD_CONV - 1
    t = np.zeros((SHIFT_ROWS, LEFT_ROWS + valid), np.float32)
    for k in range(SSD_CONV):
        for r in range(valid):
            src = r + k - keep
            if src >= 0:
                t[k * CHUNK + r, LEFT_ROWS + src] = 1.0
            else:
                for piece in range(3):
                    t[k * CHUNK + r, piece * BF16_ROWS + BF16_ROWS + src] = 1.0
    for r in range(keep):
        t[SSD_CONV * CHUNK + r, LEFT_ROWS + valid - keep + r] = 1.0
    return jnp.asarray(t, BF16)


def _ssd_stages(refs, valid, hpg, has_init, c, first, last_chunk):
    if has_init:
        (z_ref, x_ref, b_ref, c_ref, dt_ref, convl_ref, init_ref, tmat_ref, cw_ref, cb_ref, dtb_ref, alog_ref,
         dskip_ref, nw_ref, sel_ref, y_ref, state_ref, convs_ref, left, tmp, st) = refs
    else:
        (z_ref, x_ref, b_ref, c_ref, dt_ref, convl_ref, tmat_ref, cw_ref, cb_ref, dtb_ref, alog_ref,
         dskip_ref, nw_ref, sel_ref, y_ref, state_ref, convs_ref, left, tmp, st) = refs
        init_ref = None
    width = x_ref.shape[1]
    heads = width // HEAD_DIM
    gn = SSD_GROUPS * SSD_STATE
    gw = hpg * HEAD_DIM
    L = CHUNK
    keep = SSD_CONV - 1

    @_when(first, lambda: c == 0)
    def _():
        tmp[...] = jnp.zeros_like(tmp)
        tmp[BF16_ROWS - keep:BF16_ROWS, :] = convl_ref[...]
        hi, mid, lo = _split3(tmp[...])
        left[0:BF16_ROWS, :] = lo
        left[BF16_ROWS:2 * BF16_ROWS, :] = mid
        left[2 * BF16_ROWS:LEFT_ROWS, :] = hi
        if has_init:
            st[...] = init_ref[...].T
        else:
            st[...] = jnp.zeros_like(st)

    tmat = tmat_ref[...]
    bounds = ((0, width), (width, width + gn), (width + gn, width + 2 * gn))
    shifted = [_dot(tmat, jnp.concatenate([left[:, lo:hi], ref[...]], axis=0))
               for ref, (lo, hi) in zip((x_ref, b_ref, c_ref), bounds)]
    yield

    def conv(sh, lo, hi):
        acc = cb_ref[:, lo:hi] + sh[0:L] * cw_ref[0:1, lo:hi]
        for k in range(1, SSD_CONV):
            acc = acc + sh[k * L:(k + 1) * L] * cw_ref[k:k + 1, lo:hi]
        convs_ref[:, lo:hi] = sh[SSD_CONV * L:SSD_CONV * L + keep]
        return _silu(acc)

    xs, bc, cc = [conv(sh, lo, hi) for sh, (lo, hi) in zip(shifted, bounds)]

    left[0:2 * BF16_ROWS, :] = jnp.zeros((2 * BF16_ROWS, left.shape[1]), BF16)
    left[2 * BF16_ROWS:LEFT_ROWS, 0:width] = x_ref[valid - BF16_ROWS:valid, :]
    left[2 * BF16_ROWS:LEFT_ROWS, width:width + gn] = b_ref[valid - BF16_ROWS:valid, :]
    left[2 * BF16_ROWS:LEFT_ROWS, width + gn:width + 2 * gn] = c_ref[valid - BF16_ROWS:valid, :]

    row = lax.broadcasted_iota(jnp.int32, (L, LANES), 0)
    lane = lax.broadcasted_iota(jnp.int32, (L, LANES), 1)
    if valid < L:
        dt_raw = jnp.concatenate([dt_ref[...], jnp.zeros((L - valid, LANES), F32)], axis=0)
    else:
        dt_raw = dt_ref[...]
    u = dt_raw + dtb_ref[...]
    dt = jnp.maximum(u, 0.0) + jnp.log1p(jnp.exp(-jnp.abs(u)))
    dt = jnp.where((row < valid) & (lane < heads), dt, 0.0)
    a = dt * (-jnp.exp(alog_ref[...]))

    tri = jnp.where(lax.broadcasted_iota(jnp.int32, (L, L), 0) >= lax.broadcasted_iota(jnp.int32, (L, L), 1),
                    1.0, 0.0).astype(BF16)
    a_cs = _dot01_left(tri, a)
    both = _dot(jnp.concatenate([_pack3(a_cs), _pack3(dt)], axis=0), sel_ref[...])
    yield
    acs_b = both[0:L]
    dt_b = both[L:2 * L]

    s_idx = lane & (HEAD_DIM - 1)
    causal = jnp.where(row >= s_idx, 0.0, -jnp.inf)
    diag = row == s_idx
    dec = []
    for j in range(width // LANES):
        blk = acs_b[:, j * LANES:(j + 1) * LANES]
        acs_row = jnp.sum(jnp.where(diag, blk, 0.0), axis=0, keepdims=True)
        dec.append(jnp.exp(blk - acs_row + causal))
    decay = jnp.concatenate(dec, axis=1)
    last_b = acs_b[L - 1:L, :]
    e_in = jnp.exp(acs_b)
    e_out = jnp.exp(last_b - acs_b)
    e_last = jnp.exp(last_b)

    xbar = xs * dt_b
    xd16 = (xbar * e_out).astype(BF16)

    pr = lax.broadcasted_iota(jnp.int32, (2 * HEAD_DIM, 2 * HEAD_DIM), 0) >= HEAD_DIM
    pc = lax.broadcasted_iota(jnp.int32, (2 * HEAD_DIM, 2 * HEAD_DIM), 1) >= HEAD_DIM
    pair_mask = pr == pc

    zg = _silu(z_ref[...].astype(F32))
    groups = range(SSD_GROUPS)
    span = lambda g: slice(g * gw, (g + 1) * gw)
    bgs = [bc[:, g * SSD_STATE:(g + 1) * SSD_STATE].astype(BF16) for g in groups]
    cgs = [cc[:, g * SSD_STATE:(g + 1) * SSD_STATE].astype(BF16) for g in groups]
    cb_ts = [_dot_nt(cgs[g], jnp.concatenate([bgs[g]] * hpg, axis=0)) for g in groups]
    st_gs = [st[:, span(g)] for g in groups]
    y_offs = [_dot(cgs[g], st_gs[g].astype(BF16)) for g in groups]
    st_new = [_dot_tn(bgs[g], xd16[:, span(g)]) for g in groups]
    yield
    for g in groups:
        st[:, span(g)] = st_gs[g] * e_last[:, span(g)] + st_new[g]
    y_diags = []
    for g in groups:
        gmat = (cb_ts[g] * decay[:, span(g)]).astype(BF16)
        for p in range(hpg // 2):
            plo = g * gw + p * 2 * HEAD_DIM
            xp = xbar[:, plo:plo + 2 * HEAD_DIM]
            bd = jnp.where(pair_mask, jnp.concatenate([xp, xp], axis=0), 0.0).astype(BF16)
            y_diags.append(_dot(gmat[:, p * 2 * HEAD_DIM:(p + 1) * 2 * HEAD_DIM], bd))
    yield
    for g in groups:
        y_diag = jnp.concatenate(y_diags[g * (hpg // 2):(g + 1) * (hpg // 2)], axis=1)
        y = y_diag + y_offs[g] * e_in[:, span(g)] + dskip_ref[:, span(g)] * xs[:, span(g)]
        gt = y[0:valid, :] * zg[:, span(g)]
        rs = lax.rsqrt(jnp.mean(gt * gt, axis=-1, keepdims=True) + EPS)
        y_ref[:, span(g)] = (gt * rs * nw_ref[:, span(g)]).astype(y_ref.dtype)

    @_when(last_chunk is not None, lambda: c == last_chunk)
    def _():
        state_ref[...] = st[...].T


def _ssd_parts(proj, dt_raw, conv_left, init_state, lp, nb, seq, width, cols, subs):
    m = proj.shape[0]
    valid = min(CHUNK, seq)
    nc = seq // valid
    heads = width // HEAD_DIM
    hpg = heads // SSD_GROUPS
    gn = SSD_GROUPS * SSD_STATE
    cch = width + 2 * gn
    has_init = init_state is not None
    assert valid % BF16_ROWS == 0 and hpg % 2 == 0 and heads <= PACK_STRIDE and 3 * PACK_STRIDE <= LANES

    def rows(b, c):
        return b * (nc // subs) + c

    def col_spec(w, off):
        assert off % w == 0
        return pl.BlockSpec((subs * valid, w), lambda b, c: (rows(b, c), off // w))

    full = lambda shape: pl.BlockSpec(shape, lambda b, c: tuple(0 for _ in shape))
    in_specs = [col_spec(width, cols["z"]), col_spec(width, cols["x"]),
                col_spec(gn, cols["B"]), col_spec(gn, cols["C"]),
                pl.BlockSpec((subs * valid, LANES), lambda b, c: (rows(b, c), 0)),
                pl.BlockSpec((None, SSD_CONV - 1, cch), lambda b, c: (b, 0, 0))]
    args = [proj, proj, proj, proj, dt_raw, conv_left]
    if has_init:
        in_specs.append(pl.BlockSpec((None, width, SSD_STATE), lambda b, c: (b, 0, 0)))
        args.append(init_state)
    tmat = _shift_matrix(valid)
    in_specs += [full(tmat.shape), full((SSD_CONV, cch)), full((1, cch)), full((1, LANES)), full((1, LANES)),
                 full((1, width)), full((1, width)), full((LANES, width))]
    args += [tmat, lp["conv_w"], lp["conv_b"], lp["dt_bias"], lp["a_log"], lp["d_skip"], lp["ssd_norm_w"], lp["sel"]]
    stages = functools.partial(_ssd_stages, valid=valid, hpg=hpg, has_init=has_init)
    out_specs = [pl.BlockSpec((subs * valid, width), lambda b, c: (rows(b, c), 0)),
                 pl.BlockSpec((None, width, SSD_STATE), lambda b, c: (b, 0, 0)),
                 pl.BlockSpec((None, SSD_CONV - 1, cch), lambda b, c: (b, 0, 0))]
    out_shapes = [jax.ShapeDtypeStruct((m, width), BF16),
                  jax.ShapeDtypeStruct((nb, width, SSD_STATE), F32),
                  jax.ShapeDtypeStruct((nb, SSD_CONV - 1, cch), F32)]
    scratch = [pltpu.VMEM((LEFT_ROWS, cch), BF16),
               pltpu.VMEM((BF16_ROWS, cch), F32),
               pltpu.VMEM((SSD_STATE, width), F32)]
    return stages, in_specs, args, out_specs, out_shapes, scratch


def _head_rms(x, w):
    return x * lax.rsqrt(jnp.mean(x * x, axis=-1, keepdims=True) + EPS) * w


def _attn_stages(refs, lq, qpk, hb, prompt, b, c, first, cur):
    if prompt:
        q_ref, ko_ref, vo_ref = refs[:3]
        rest = refs[3:]
    else:
        q_ref, ko_ref, vo_ref, kp_ref, vp_ref = refs[:5]
        rest = refs[5:]
    kw_ref, ks_ref, sink_ref, dup_ref, eye_ref, mq_ref, mo_ref, ep_ref, o_ref, kn_ref, krings, vrings, bias = rest
    nk = WINDOW + lq
    heads = KV_HEADS * qpk
    wc = WINDOW // CHUNK
    kring = krings.at[cur]
    vring = vrings.at[cur]

    @_when(first, lambda: (b == 0) & (c == 0))
    def _():
        j = lax.broadcasted_iota(jnp.int32, (nk, heads * lq), 1)
        s = lax.broadcasted_iota(jnp.int32, (nk, heads * lq), 0)
        h = lax.shift_right_logical(j, lq.bit_length() - 1)
        l_idx = j & (lq - 1)
        slope = jnp.exp2(-8.0 * (h + 1).astype(F32) / heads)
        bias[...] = -(slope * LOG2E) * jnp.abs(WINDOW + l_idx - s).astype(F32)

    if prompt:
        @_when(first, lambda: c == 0)
        def _():
            krings[...] = jnp.zeros_like(krings)
            vrings[...] = jnp.zeros_like(vrings)
    else:
        kring[0:WINDOW, :] = (kp_ref[...] * ks_ref[...]).astype(BF16)
        vring[0:WINDOW, :] = vp_ref[...].astype(BF16)

    yield
    kw = kw_ref[...]
    for g in range(KV_HEADS):
        lo, hi = g * HEAD_DIM, (g + 1) * HEAD_DIM
        kn_ref[:, lo:hi] = _head_rms(ko_ref[:, lo:hi].astype(F32), kw)
    kring[WINDOW:nk, :] = (kn_ref[...] * ks_ref[...]).astype(BF16)
    vring[WINDOW:nk, :] = vo_ref[...].astype(BF16)

    hw = hb * HEAD_DIM
    pw = hb * lq
    kdup = _dot(kring[...], dup_ref[...]).astype(BF16)
    vt = _dot_nt(eye_ref[...], vring[...]).astype(BF16)
    mask_q = mq_ref[...]
    mask_o = mo_ref[...]
    eye_blk = ep_ref[...]
    ones = jnp.ones((SUBLANES, hw), BF16)
    yield
    if prompt:
        offs = [jnp.where(c >= wc - i, 0.0, jnp.inf) for i in range(wc)]
    pairs = range(heads // hb)
    group = lambda pi: pi // (qpk // hb)
    qbs = []
    for pi in pairs:
        qp = q_ref[:, pi * hw:(pi + 1) * hw]
        qbs.append(jnp.concatenate([qp] * hb, axis=0) * mask_q)
    ssqs = [_dot_nt(ones, qb * qb)[0:1] for qb in qbs]
    raw = [_dot_nt(kdup[:, group(pi) * hw:(group(pi) + 1) * hw], qbs[pi]) for pi in pairs]
    yield
    pns = []
    for pi in pairs:
        rq = lax.rsqrt(ssqs[pi] * (1.0 / HEAD_DIM) + EPS)
        s2 = raw[pi] * rq + bias[:, pi * pw:(pi + 1) * pw]
        sink2 = sink_ref[:, pi * pw:(pi + 1) * pw]
        m_own = jnp.maximum(jnp.max(s2[WINDOW:nk], axis=0, keepdims=True), sink2)
        if prompt:
            mx = m_own
            for i in range(wc):
                band = s2[i * CHUNK:(i + 1) * CHUNK]
                mx = jnp.maximum(mx, jnp.max(band, axis=0, keepdims=True) - offs[i])
            p = jnp.concatenate([jnp.exp2(s2[i * CHUNK:(i + 1) * CHUNK] - (mx + offs[i])) for i in range(wc)]
                                + [jnp.exp2(s2[WINDOW:nk] - mx)], axis=0)
        else:
            mx = jnp.maximum(m_own, jnp.max(s2[0:WINDOW], axis=0, keepdims=True))
            p = jnp.exp2(s2 - mx)
        den = jnp.sum(p, axis=0, keepdims=True) + jnp.exp2(sink2 - mx)
        pns.append((p * (1.0 / den)).astype(BF16))
    ots = [_dot(vt[group(pi) * HEAD_DIM:(group(pi) + 1) * HEAD_DIM, :], pns[pi]).astype(BF16)
           for pi in pairs]
    yield
    obs = [jnp.concatenate([ot] * hb, axis=0) * mask_o for ot in ots]
    outs = [_dot_nt(eye_blk, ob) for ob in obs]
    for pi in pairs:
        o_ref[:, pi * hw:(pi + 1) * hw] = outs[pi].astype(o_ref.dtype)

    if prompt:
        nxt = lax.rem(cur + 1, WINDOW_BUFFERS)
        krings.at[nxt][0:WINDOW, :] = kring[lq:nk, :]
        vrings.at[nxt][0:WINDOW, :] = vring[lq:nk, :]


def _attn_parts(proj, cols, cache_kv, lp, nb, seq, width, subs):
    m = nb * seq
    lq = min(CHUNK, seq)
    nc = seq // lq
    heads = width // HEAD_DIM
    qpk = heads // KV_HEADS
    kvw = KV_HEADS * HEAD_DIM
    prompt = cache_kv is None
    nk = WINDOW + lq
    assert qpk % 2 == 0 and lq & (lq - 1) == 0 and lq % BF16_ROWS == 0 and WINDOW % CHUNK == 0
    assert cols["q"] % width == 0 and cols["k"] % kvw == 0 and cols["v"] % kvw == 0
    assert prompt and lq == CHUNK or not prompt and nc == 1
    row_spec = lambda w, off: pl.BlockSpec((subs * lq, w), lambda b, c: (b * (nc // subs) + c, off // w))
    full = lambda shape: pl.BlockSpec(shape, lambda b, c: tuple(0 for _ in shape))
    in_specs = [row_spec(width, cols["q"]), row_spec(kvw, cols["k"]), row_spec(kvw, cols["v"])]
    args = [proj, proj, proj]
    if not prompt:
        in_specs += [pl.BlockSpec((WINDOW, kvw), lambda b, c: (b, 0))] * 2
        args += list(cache_kv)
    sink2 = (jnp.repeat(lp["sinks"], lq) * LOG2E).reshape(1, heads * lq)
    hb = MXU_WIDTH // HEAD_DIM if qpk % (MXU_WIDTH // HEAD_DIM) == 0 else 2
    hw = hb * HEAD_DIM
    kv_idx = np.arange(kvw)
    dup_idx = np.arange(KV_HEADS * hw)
    dup = (kv_idx[:, None] // HEAD_DIM == dup_idx[None, :] // hw) & (kv_idx[:, None] % HEAD_DIM == dup_idx[None, :] % HEAD_DIM)
    ql = np.arange(hb * lq)
    dl = np.arange(hw)
    mask_q = ql[:, None] // lq == dl[None, :] // HEAD_DIM
    eye_blk = np.arange(lq)[:, None] == ql[None, :] % lq
    consts = [jnp.asarray(a, BF16) for a in (dup, np.eye(kvw), mask_q, mask_q.T, eye_blk)]
    in_specs += [full((1, HEAD_DIM)), full((1, kvw)), full((1, heads * lq))] + [full(a.shape) for a in consts]
    args += [lp["k_norm_w"], lp["k_scale"], sink2] + consts
    stages = functools.partial(_attn_stages, lq=lq, qpk=qpk, hb=hb, prompt=prompt)
    out_specs = [pl.BlockSpec((subs * lq, width), lambda b, c: (b * (nc // subs) + c, 0)),
                 pl.BlockSpec((subs * lq, kvw), lambda b, c: (b * (nc // subs) + c, 0))]
    out_shapes = [jax.ShapeDtypeStruct((m, width), BF16),
                  jax.ShapeDtypeStruct((m, kvw), F32)]
    scratch = [pltpu.VMEM((WINDOW_BUFFERS, nk, kvw), BF16),
               pltpu.VMEM((WINDOW_BUFFERS, nk, kvw), BF16),
               pltpu.VMEM((nk, heads * lq), F32)]
    return stages, in_specs, args, out_specs, out_shapes, scratch


SSD_ROW_REFS = ((0, 1, 2, 3, 4), (0,))
ATTN_ROW_REFS = ((0, 1, 2), (0, 1))


def _mixer_kernel(*refs, ssd_stages, attn_stages, n_ssd, n_attn, n_cast, subs, nc, rows):
    (si, so, ss), (ai, ao, as_) = n_ssd, n_attn
    n_in, n_out = si + ai + n_cast, so + ao + n_cast
    ins, outs, scr = refs[:n_in], refs[n_in:n_in + n_out], refs[n_in + n_out:]
    b = pl.program_id(0)

    def chunk_rows(group, which, s):
        return tuple(r.at[s * rows:(s + 1) * rows] if i in which else r for i, r in enumerate(group))

    for s in range(subs):
        c = pl.program_id(1) * subs + s
        ssd = ssd_stages(chunk_rows(ins[:si], SSD_ROW_REFS[0], s) + chunk_rows(outs[:so], SSD_ROW_REFS[1], s) + scr[:ss],
                         c=c, first=s == 0, last_chunk=nc - 1 if s == subs - 1 else None)
        attn = attn_stages(chunk_rows(ins[si:si + ai], ATTN_ROW_REFS[0], s)
                           + chunk_rows(outs[so:so + ao], ATTN_ROW_REFS[1], s) + scr[ss:],
                           b=b, c=c, first=s == 0, cur=lax.rem(c, WINDOW_BUFFERS))
        for gen in (attn, ssd, attn, attn, ssd, ssd, attn, ssd, attn, ssd):
            next(gen, None)
        for gen in (ssd, attn):
            assert next(gen, "done") == "done"
    _cast_slabs(ins[si + ai:], outs[so + ao:])


def _mixer(proj, dt_raw, conv_left, init_state, cache_kv, lp, nb, seq, width, cols, cast_ws=()):
    rows = min(CHUNK, seq)
    nc = seq // rows
    subs = SUB_CHUNKS if nc % SUB_CHUNKS == 0 else 1
    steps = nc // subs
    s_st, s_in, s_args, s_out, s_shape, s_scr = _ssd_parts(proj, dt_raw, conv_left, init_state, lp, nb, seq, width, cols, subs)
    a_st, a_in, a_args, a_out, a_shape, a_scr = _attn_parts(proj, cols, cache_kv, lp, nb, seq, width, subs)
    slabs = _cast_slab_specs(cast_ws, nb * steps, lambda b, c: b * steps + c) if cast_ws else []
    if slabs is None:
        slabs, late = [], [w_.astype(BF16) for w_ in cast_ws]
    else:
        late = None
    cast_args = list(cast_ws) if slabs else []
    kern = functools.partial(_mixer_kernel, ssd_stages=s_st, attn_stages=a_st, n_cast=len(slabs), subs=subs, nc=nc, rows=rows,
                             n_ssd=(len(s_in), len(s_out), len(s_scr)), n_attn=(len(a_in), len(a_out), len(a_scr)))
    outs = pl.pallas_call(
        kern,
        grid=(nb, steps),
        in_specs=s_in + a_in + slabs,
        out_specs=s_out + a_out + slabs,
        out_shape=s_shape + a_shape + [jax.ShapeDtypeStruct(w_.shape, BF16) for w_ in cast_args],
        scratch_shapes=s_scr + a_scr,
        compiler_params=_cparams("arbitrary", "arbitrary"),
        name="mixer",
    )(*s_args, *a_args, *cast_args)
    n_main = len(s_out) + len(a_out)
    return outs[:n_main], (late if late is not None else list(outs[n_main:]))


def _prep_layer(l, w_ada, b_ada, g_mix, w_in, conv_w, conv_b, dt_bias, a_log, d_skip, ssd_norm_w,
                q_norm_w, k_norm_w, sinks, w_out, g_ffn, w_gate_up, w_down):
    d = w_in.shape[1]
    width = d // 2
    gn = SSD_GROUPS * SSD_STATE
    kvw = KV_HEADS * HEAD_DIM
    heads = width // HEAD_DIM
    wi = w_in[l]
    o1 = width
    o2 = o1 + width + 2 * gn
    o3 = o2 + heads
    w_all = wi.astype(BF16)
    w_b = w_all[:, o3:]
    w_dt = jnp.pad(w_all[:, o2:o3], ((0, 0), (0, LANES - heads)))
    cols = {"z": 0, "x": width, "q": 2 * width, "B": 3 * width, "C": 3 * width + gn,
            "k": 3 * width + 2 * gn, "v": 3 * width + 2 * gn + kvw}
    tn = 2 * kvw
    assert width % tn == 0 and gn % tn == 0
    src_order = ["z", "x", "B", "C", "q", "k"]
    src_width = {"z": width, "x": width, "B": gn, "C": gn, "q": width, "k": 2 * kvw}
    dest = [cols[name] // tn + t for name in src_order for t in range(src_width[name] // tn)]
    lane_piece = jnp.arange(LANES)
    head_of_lane = jnp.arange(width) // HEAD_DIM
    sel = ((lane_piece[:, None] % PACK_STRIDE == head_of_lane[None, :])
           & (lane_piece[:, None] < 3 * PACK_STRIDE)).astype(BF16)
    pad_h = (0, LANES - heads)
    k_scale = jnp.tile(q_norm_w[l], KV_HEADS) * (HEAD_DIM ** -0.5 * LOG2E)
    return dict(
        w_ada=w_ada[l], b_ada=b_ada[l], g_mix=g_mix[l], g_ffn=g_ffn[l],
        w_a=w_all, a_cols=o2, w_b=w_b, w_dt=w_dt, cols=cols, dest=dest, tn_in=tn,
        w_out=w_out[l], w_gu=w_gate_up[l], w_dn=w_down[l],
        conv_w=conv_w[l], conv_b=conv_b[l].reshape(1, -1),
        dt_bias=jnp.pad(dt_bias[l], pad_h).reshape(1, LANES), a_log=jnp.pad(a_log[l], pad_h).reshape(1, LANES),
        d_skip=jnp.repeat(d_skip[l], HEAD_DIM).reshape(1, width), ssd_norm_w=ssd_norm_w[l].reshape(1, width),
        sel=sel, k_norm_w=k_norm_w[l].reshape(1, HEAD_DIM), k_scale=k_scale.reshape(1, kvw),
        sinks=sinks[l], width=width)


def _layer(x, mods, boff, lp, ssd_init, conv_left, cache_kv, w16=None):
    nb, seq, d = x.shape
    m = nb * seq
    width = lp["width"]
    cols = lp["cols"]
    kvw = KV_HEADS * HEAD_DIM
    sh1, sc1, g1, sh2, sc2, g2 = mods
    x2 = x.reshape(m, d)
    h = _norm_mod(x2, lp["g_mix"], sc1, sh1, seq, boff)
    proj, dt_raw = _mm_in(h, lp["w_a"], lp["a_cols"], lp["w_b"], lp["w_dt"], lp["dest"], lp["tn_in"])
    if cache_kv is not None:
        cache_kv = tuple(t.reshape(nb * WINDOW, kvw) for t in cache_kv)
    make = w16 is None
    (y, ssd_state, conv_state, o, kn), made = _mixer(proj, dt_raw, conv_left, ssd_init, cache_kv, lp, nb, seq, width,
                                                      cols, (lp["w_out"], lp["w_gu"]) if make else ())
    w_out16, w_gu16 = made if make else (w16["w_out"], w16["w_gu"])
    x1 = _mm_out(y, o, w_out16, x2, g1, seq, boff)
    h2 = _norm_mod(x1, lp["g_ffn"], sc2, sh2, seq, boff)
    act, made = _mm_gate_up(h2, w_gu16, (lp["w_dn"],) if make else ())
    w_dn16 = made[0] if make else w16["w_dn"]
    out = _mm_down(act, w_dn16, x1, g2, seq, boff)
    keep = min(WINDOW, seq) if cache_kv is None else seq
    heads = width // HEAD_DIM
    k_state = kn.reshape(nb, seq, KV_HEADS, HEAD_DIM)[:, seq - keep:]
    v_state = proj[:, cols["v"]:cols["v"] + kvw].astype(F32).reshape(nb, seq, KV_HEADS, HEAD_DIM)[:, seq - keep:]
    ssd_state = ssd_state.reshape(nb, heads, HEAD_DIM, SSD_STATE)
    w16 = dict(w_out=w_out16, w_gu=w_gu16, w_dn=w_dn16)
    return out.reshape(nb, seq, d), ssd_state, conv_state, k_state, v_state, w16


def kernel(x_prompt, x_sample, state_ssd, state_conv, cache_k, cache_v, c_prompt, c_sample, w_ada, b_ada, g_mix, w_in, conv_w, conv_b, dt_bias, a_log, d_skip, ssd_norm_w, q_norm_w, k_norm_w, sinks, w_out, g_ffn, w_gate_up, w_down):
    depth = w_in.shape[0]
    bp, _, d = x_prompt.shape
    bs = x_sample.shape[0]
    width = d // 2
    assert cache_k.shape[2] == WINDOW and x_sample.shape[1] <= CHUNK
    yp, ys = x_prompt, x_sample
    outs = [[] for _ in range(8)]
    c_all = jnp.concatenate([c_prompt, c_sample], axis=0)
    zero_conv = jnp.zeros((bp, SSD_CONV - 1, width + 2 * SSD_GROUPS * SSD_STATE), F32)
    for l in range(depth):
        lp = _prep_layer(l, w_ada, b_ada, g_mix, w_in, conv_w, conv_b, dt_bias, a_log, d_skip, ssd_norm_w,
                         q_norm_w, k_norm_w, sinks, w_out, g_ffn, w_gate_up, w_down)
        mod = _ada(c_all, lp["w_ada"], lp["b_ada"]).reshape(bp + bs, N_MOD, 1, d)
        mods = [mod[:, i] for i in range(N_MOD)]
        yp, s1, s2, s3, s4, w16 = _layer(yp, mods, 0, lp, None, zero_conv, None)
        init = state_ssd[l].reshape(bs, width, SSD_STATE)
        ys, t1, t2, t3, t4, _ = _layer(ys, mods, bp, lp, init, state_conv[l], (cache_k[l], cache_v[l]), w16)
        for lst, v in zip(outs, (s1, s2, s3, s4, t1, t2, t3, t4)):
            lst.append(v)
    stacked = [jnp.stack(v, axis=0) for v in outs]
    return (yp, ys, *stacked)
```
